```python
import math
import jax, jax.numpy as jnp
from jax import lax
import numpy as np

D_MODEL = 1024
BATCH = 16
SEQ = 2048
DEPTH = 2

HEAD_DIM = 64
MIX_WIDTH = D_MODEL
HG_WIDTH = MIX_WIDTH // 4
S5_WIDTH = MIX_WIDTH // 4
ATTN_WIDTH = MIX_WIDTH - HG_WIDTH - S5_WIDTH
HG_HEADS = HG_WIDTH // HEAD_DIM
HG_CHUNK = 16
ATTN_HEADS = ATTN_WIDTH // HEAD_DIM
KV_HEADS = 2
KV_WIDTH = KV_HEADS * HEAD_DIM
Q_BLOCK = 128
ROPE_THETA = 10000.0
GRID_W = 64
S5_CH_PER_GROUP = 16
S5_GROUPS = S5_WIDTH // S5_CH_PER_GROUP
S5_STATE = 64
N_EXPERT_GROUPS = 4
EXPERTS_PER_GROUP = 8
N_EXPERTS = N_EXPERT_GROUPS * EXPERTS_PER_GROUP
TOP_K_IN_GROUP = 2
EXPERT_FF = D_MODEL // 2
MOE_BLOCK = 128
NORM_EPS = 1e-6

IN_SPLITS = (HG_WIDTH, HG_WIDTH, HG_WIDTH, HG_WIDTH, HG_WIDTH, ATTN_WIDTH, KV_WIDTH, KV_WIDTH, S5_WIDTH)
IN_WIDTH = sum(IN_SPLITS)
IN_OFFSETS = tuple(sum(IN_SPLITS[:i + 1]) for i in range(len(IN_SPLITS) - 1))

kernel_name = "hybrid_hgrn2_gqa_s5_hmoe_encoder"


def rms_norm(x, g):
    xf = x.astype(jnp.float32)
    y = xf * lax.rsqrt(jnp.mean(xf * xf, axis=-1, keepdims=True) + NORM_EPS)
    return (y * g.astype(jnp.float32)).astype(x.dtype)


def split_heads(t, n_heads):
    B, S, _ = t.shape
    return t.reshape(B, S, n_heads, -1).transpose(0, 2, 1, 3)


def gla_chunkwise(q, k, v, log_f):
    B, H, S, dk = q.shape
    dv = v.shape[-1]
    n_chunks = S // HG_CHUNK
    rs = lambda t: t.reshape(B, H, n_chunks, HG_CHUNK, t.shape[-1])
    q, k, v, log_f = rs(q), rs(k), rs(v), rs(log_f)
    b = jnp.cumsum(log_f, axis=3)
    b_last = b[:, :, :, -1:, :]
    q_dec = q * jnp.exp(b)
    k_dec = k * jnp.exp(-b)
    scores = jnp.einsum('bhntd,bhnsd->bhnts', q_dec, k_dec)
    mask = jnp.tril(jnp.ones((HG_CHUNK, HG_CHUNK), dtype=bool))
    scores = jnp.where(mask, scores, 0.0)
    o_intra = jnp.einsum('bhnts,bhnsv->bhntv', scores, v)
    k_end = k * jnp.exp(b_last - b)
    chunk_kv = jnp.einsum('bhntd,bhntv->bhndv', k_end, v)
    chunk_decay = jnp.exp(b_last[:, :, :, 0, :])

    def step(state, inp):
        dec, kv = inp
        return dec[..., None] * state + kv, state

    _, s_start = lax.scan(step, jnp.zeros((B, H, dk, dv), jnp.float32),
                          (jnp.moveaxis(chunk_decay, 2, 0), jnp.moveaxis(chunk_kv, 2, 0)))
    s_start = jnp.moveaxis(s_start, 0, 2)
    o_inter = jnp.einsum('bhntd,bhndv->bhntv', q_dec, s_start)
    return (o_intra + o_inter).reshape(B, H, S, dv)


def hgrn2_mixer(q, i, g, zf, zb, lb, norm_g):
    B, S, _ = q.shape
    qh = split_heads(jax.nn.silu(q.astype(jnp.float32)), HG_HEADS)
    vh = split_heads(i.astype(jnp.float32), HG_HEADS)

    def log_forget(z, lbd):
        return jnp.logaddexp(jnp.log1p(-lbd) + jax.nn.log_sigmoid(z.astype(jnp.float32)), jnp.log(lbd))

    lf_f = split_heads(log_forget(zf, lb[0]), HG_HEADS)
    lf_b = split_heads(log_forget(zb, lb[1]), HG_HEADS)
    flip = lambda t: jnp.flip(t, axis=2)
    o_f = gla_chunkwise(qh, -jnp.expm1(lf_f), vh, lf_f)
    o_b = flip(gla_chunkwise(flip(qh), flip(-jnp.expm1(lf_b)), flip(vh), flip(lf_b)))
    o = (o_f + o_b).transpose(0, 2, 1, 3)
    gate = jax.nn.silu(g.astype(jnp.float32)).reshape(B, S, HG_HEADS, HEAD_DIM)
    o = rms_norm(o, norm_g) * gate
    return o.reshape(B, S, HG_WIDTH).astype(q.dtype)


def axial_rope_tables(S):
    rows_count = S // GRID_W
    rows = jnp.repeat(jnp.arange(rows_count, dtype=jnp.float32), GRID_W)
    cols = jnp.tile(jnp.arange(GRID_W, dtype=jnp.float32), rows_count)
    pairs = HEAD_DIM // 4
    freqs = jnp.power(jnp.float32(ROPE_THETA), -jnp.arange(pairs, dtype=jnp.float32) / pairs)
    ang = jnp.concatenate([rows[:, None] * freqs, cols[:, None] * freqs], axis=-1)
    return jnp.cos(ang), jnp.sin(ang)


def apply_rope(x, cos, sin):
    xf = x.astype(jnp.float32).reshape(*x.shape[:-1], HEAD_DIM // 2, 2)
    x0, x1 = xf[..., 0], xf[..., 1]
    c, s = cos[None, :, None, :], sin[None, :, None, :]
    return jnp.stack([x0 * c - x1 * s, x0 * s + x1 * c], axis=-1).reshape(x.shape)


def gqa_attention(q, k, v, q_gain, k_gain, cos, sin):
    B, S, _ = q.shape
    G = ATTN_HEADS // KV_HEADS
    qh = apply_rope(rms_norm(q.reshape(B, S, ATTN_HEADS, HEAD_DIM), q_gain), cos, sin)
    kh = apply_rope(rms_norm(k.reshape(B, S, KV_HEADS, HEAD_DIM), k_gain), cos, sin).transpose(0, 2, 1, 3)
    vh = v.reshape(B, S, KV_HEADS, HEAD_DIM).astype(jnp.float32).transpose(0, 2, 1, 3)
    n_blocks = S // Q_BLOCK
    qb = qh.reshape(B, n_blocks, Q_BLOCK, KV_HEADS, G, HEAD_DIM).transpose(1, 0, 3, 4, 2, 5)
    scale = HEAD_DIM ** -0.5

    def attend(q_blk):
        s = jnp.einsum('bkgqd,bksd->bkgqs', q_blk, kh) * scale
        p = jax.nn.softmax(s, axis=-1)
        return jnp.einsum('bkgqs,bksd->bkgqd', p, vh)

    o = lax.map(attend, qb)
    return o.transpose(1, 0, 4, 2, 3, 5).reshape(B, S, ATTN_WIDTH).astype(q.dtype)


def _linear_combine(left, right):
    a_l, b_l = left
    a_r, b_r = right
    return a_r * a_l, a_r * b_l + b_r


def s5_mixer(u, a_re, a_im, log_dt, b_re, b_im, c_re, c_im, d_skip, w_glu, b_glu):
    B, S, _ = u.shape
    uf = u.astype(jnp.float32)
    ug = uf.reshape(B, S, S5_GROUPS, S5_CH_PER_GROUP).astype(jnp.complex64)
    y = d_skip.astype(jnp.float32) * uf
    for d in range(2):
        lam = lax.complex(a_re[d].astype(jnp.float32), a_im[d].astype(jnp.float32))
        dt = jnp.exp(log_dt[d].astype(jnp.float32))[:, None]
        lam_bar = jnp.exp(lam * dt)
        b_mat = lax.complex(b_re[d].astype(jnp.float32), b_im[d].astype(jnp.float32))
        b_bar = ((lam_bar - 1.0) / lam)[..., None] * b_mat
        bu = jnp.einsum('bsgc,gpc->sbgp', ug, b_bar)
        a = jnp.broadcast_to(lam_bar[None, None], (S, 1, S5_GROUPS, S5_STATE))
        _, states = lax.associative_scan(_linear_combine, (a, bu), axis=0, reverse=(d == 1))
        c_mat = lax.complex(c_re[d].astype(jnp.float32), c_im[d].astype(jnp.float32))
        y = y + jnp.real(jnp.einsum('sbgp,gcp->bsgc', states, c_mat)).reshape(B, S, S5_WIDTH)
    z = jax.nn.gelu(y)
    out = z * jax.nn.sigmoid(z @ w_glu.astype(jnp.float32) + b_glu.astype(jnp.float32))
    return out.astype(u.dtype)


def hier_moe(h, w_rg, b_rg, w_re, b_re, w_gate, w_up, w_down):
    B, S, D = h.shape
    T = B * S
    hf = h.reshape(T, D)
    tok = jnp.arange(T, dtype=jnp.int32)
    g_logits = (hf @ w_rg + b_rg).astype(jnp.float32)
    g_prob = jax.nn.softmax(g_logits, axis=-1)
    g_sel = jnp.argmax(g_logits, axis=-1).astype(jnp.int32)
    p_group = g_prob[tok, g_sel]
    e_logits = (hf @ w_re + b_re).astype(jnp.float32).reshape(T, N_EXPERT_GROUPS, EXPERTS_PER_GROUP)
    top_val, top_idx = lax.top_k(e_logits[tok, g_sel], TOP_K_IN_GROUP)
    gate = p_group[:, None] * jax.nn.softmax(top_val, axis=-1)
    expert_id = g_sel[:, None] * EXPERTS_PER_GROUP + top_idx.astype(jnp.int32)

    n_assign = T * TOP_K_IN_GROUP
    flat_e = expert_id.reshape(n_assign)
    flat_w = gate.reshape(n_assign)
    flat_tok = jnp.repeat(tok, TOP_K_IN_GROUP)
    order = jnp.argsort(flat_e)
    sorted_e, sorted_tok, sorted_w = flat_e[order], flat_tok[order], flat_w[order]
    counts = jnp.bincount(flat_e, length=N_EXPERTS)
    padded = ((counts + MOE_BLOCK - 1) // MOE_BLOCK) * MOE_BLOCK
    starts = jnp.cumsum(counts) - counts
    pends = jnp.cumsum(padded)
    pstarts = pends - padded
    dest = pstarts[sorted_e] + (jnp.arange(n_assign, dtype=jnp.int32) - starts[sorted_e])
    n_blocks = -(-n_assign // MOE_BLOCK) + N_EXPERTS
    n_rows = n_blocks * MOE_BLOCK
    buf_tok = jnp.full((n_rows,), T, jnp.int32).at[dest].set(sorted_tok)
    buf_w = jnp.zeros((n_rows,), jnp.float32).at[dest].set(sorted_w)
    blk_e = jnp.minimum(jnp.searchsorted(pends, jnp.arange(n_blocks, dtype=jnp.int32) * MOE_BLOCK, side='right'),
                        N_EXPERTS - 1)
    x_pad = jnp.concatenate([hf, jnp.zeros((1, D), hf.dtype)], axis=0)

    def expert_block(args):
        rows, e = args
        xb = x_pad[rows]
        return (jax.nn.silu(xb @ w_gate[e]) * (xb @ w_up[e])) @ w_down[e]

    ys = lax.map(expert_block, (buf_tok.reshape(n_blocks, MOE_BLOCK), blk_e))
    ys = ys.reshape(n_rows, D) * buf_w[:, None].astype(ys.dtype)
    out = jax.ops.segment_sum(ys, buf_tok, num_segments=T + 1)[:T]
    return out.reshape(B, S, D)


def setup_inputs(seed: int = 0) -> dict:
    key = jax.random.key(seed)
    ks = jax.random.split(key, 32)
    f32 = jnp.float32
    nrm = lambda k, shape, scale: scale * jax.random.normal(k, shape, f32)
    L = DEPTH
    return {
        "x": nrm(ks[0], (BATCH, SEQ, D_MODEL), 1.0),
        "norm1_g": 1.0 + nrm(ks[1], (L, D_MODEL), 0.02),
        "w_in": nrm(ks[2], (L, D_MODEL, IN_WIDTH), D_MODEL ** -0.5),
        "hgrn_lower_bounds": nrm(ks[3], (L, 2, HG_WIDTH), 0.5),
        "hgrn_norm_g": 1.0 + nrm(ks[4], (L, HEAD_DIM), 0.02),
        "attn_q_norm_g": 1.0 + nrm(ks[5], (L, HEAD_DIM), 0.02),
        "attn_k_norm_g": 1.0 + nrm(ks[6], (L, HEAD_DIM), 0.02),
        "s5_a_re": -0.5 + nrm(ks[7], (L, 2, S5_GROUPS, S5_STATE), 0.01),
        "s5_a_im": jnp.pi * jnp.arange(S5_STATE, dtype=f32) + nrm(ks[8], (L, 2, S5_GROUPS, S5_STATE), 0.01),
        "s5_log_dt": jax.random.uniform(ks[9], (L, 2, S5_GROUPS), f32, math.log(1e-3), math.log(1e-1)),
        "s5_b_re": nrm(ks[10], (L, 2, S5_GROUPS, S5_STATE, S5_CH_PER_GROUP), (2 * S5_CH_PER_GROUP) ** -0.5),
        "s5_b_im": nrm(ks[11], (L, 2, S5_GROUPS, S5_STATE, S5_CH_PER_GROUP), (2 * S5_CH_PER_GROUP) ** -0.5),
        "s5_c_re": nrm(ks[12], (L, 2, S5_GROUPS, S5_CH_PER_GROUP, S5_STATE), 0.5),
        "s5_c_im": nrm(ks[13], (L, 2, S5_GROUPS, S5_CH_PER_GROUP, S5_STATE), 0.5),
        "s5_d": nrm(ks[14], (L, S5_WIDTH), 1.0),
        "s5_w_glu": nrm(ks[15], (L, S5_WIDTH, S5_WIDTH), S5_WIDTH ** -0.5),
        "s5_b_glu": nrm(ks[16], (L, S5_WIDTH), 0.01),
        "w_out": nrm(ks[17], (L, MIX_WIDTH, D_MODEL), MIX_WIDTH ** -0.5),
        "norm2_g": 1.0 + nrm(ks[18], (L, D_MODEL), 0.02),
        "router_group_w": nrm(ks[19], (L, D_MODEL, N_EXPERT_GROUPS), D_MODEL ** -0.5),
        "router_group_b": nrm(ks[20], (L, N_EXPERT_GROUPS), 0.01),
        "router_expert_w": nrm(ks[21], (L, D_MODEL, N_EXPERTS), D_MODEL ** -0.5),
        "router_expert_b": nrm(ks[22], (L, N_EXPERTS), 0.01),
        "expert_w_gate": nrm(ks[23], (L, N_EXPERTS, D_MODEL, EXPERT_FF), D_MODEL ** -0.5),
        "expert_w_up": nrm(ks[24], (L, N_EXPERTS, D_MODEL, EXPERT_FF), D_MODEL ** -0.5),
        "expert_w_down": nrm(ks[25], (L, N_EXPERTS, EXPERT_FF, D_MODEL), EXPERT_FF ** -0.5),
        "final_norm_g": 1.0 + nrm(ks[26], (D_MODEL,), 0.02),
    }


def reference(x, norm1_g, w_in, hgrn_lower_bounds, hgrn_norm_g, attn_q_norm_g, attn_k_norm_g,
              s5_a_re, s5_a_im, s5_log_dt, s5_b_re, s5_b_im, s5_c_re, s5_c_im, s5_d, s5_w_glu, s5_b_glu,
              w_out, norm2_g, router_group_w, router_group_b, router_expert_w, router_expert_b,
              expert_w_gate, expert_w_up, expert_w_down, final_norm_g):
    S = x.shape[1]
    cos, sin = axial_rope_tables(S)
    lb_all = jnp.cumsum(jax.nn.softmax(hgrn_lower_bounds.astype(jnp.float32), axis=0), axis=0)
    lb_all = lb_all - lb_all[0:1]
    for l in range(DEPTH):
        h = rms_norm(x, norm1_g[l])
        proj = h @ w_in[l]
        hg_q, hg_i, hg_g, hg_ff, hg_fb, a_q, a_k, a_v, s5_u = jnp.split(proj, IN_OFFSETS, axis=-1)
        o_a = hgrn2_mixer(hg_q, hg_i, hg_g, hg_ff, hg_fb, lb_all[l], hgrn_norm_g[l])
        o_b = gqa_attention(a_q, a_k, a_v, attn_q_norm_g[l], attn_k_norm_g[l], cos, sin)
        o_c = s5_mixer(s5_u, s5_a_re[l], s5_a_im[l], s5_log_dt[l], s5_b_re[l], s5_b_im[l],
                       s5_c_re[l], s5_c_im[l], s5_d[l], s5_w_glu[l], s5_b_glu[l])
        mixed = jnp.concatenate([o_a, o_b, o_c], axis=-1)
        x = x + (mixed @ w_out[l]).astype(x.dtype)
        h = rms_norm(x, norm2_g[l])
        x = x + hier_moe(h, router_group_w[l], router_group_b[l], router_expert_w[l], router_expert_b[l],
                         expert_w_gate[l], expert_w_up[l], expert_w_down[l]).astype(x.dtype)
    return rms_norm(x, final_norm_g)
```

```python
import functools
import math

import jax
import jax.numpy as jnp
from jax import lax
from jax.experimental import pallas as pl
from jax.experimental.pallas import tpu as pltpu

F32 = jnp.float32
BF16 = jnp.bfloat16
HIGHEST = lax.Precision.HIGHEST

D_MODEL = 1024
HEAD_DIM = 64
HG_WIDTH = 256
S5_WIDTH = 256
ATTN_WIDTH = 512
KV_WIDTH = 128
HG_HEADS = HG_WIDTH // HEAD_DIM
HG_CHUNK = 16
ROPE_THETA = 10000.0
GRID_W = 64
S5_CH = 16
S5_GROUPS = S5_WIDTH // S5_CH
S5_STATE = 64
N_GROUPS = 4
PER_GROUP = 8
N_EXPERTS = N_GROUPS * PER_GROUP
TOP_K = 2
EXPERT_FF = D_MODEL // 2
NORM_EPS = 1e-6
HG_COLS = 5 * HG_WIDTH
QKV_COLS = ATTN_WIDTH + 2 * KV_WIDTH
IN_WIDTH = HG_COLS + QKV_COLS + S5_WIDTH
S5_REAL = 2 * S5_GROUPS * S5_STATE

LANES = 128
HG_BLOCK = 128
ATTN_QB = 128
S5_LT = 32
ROW_TILE = 512
MOE_BM = 256
GATHER_ROWS = 512
ROUTER_LANES = 128
VMEM_LIMIT = 56 * 1024 * 1024


def _dot(a, b):
    return jnp.dot(a, b, preferred_element_type=F32)


def _dot_hi(a, b):
    return jnp.dot(a, b, preferred_element_type=F32, precision=HIGHEST)


def _dot_nt(a, b):
    return lax.dot_general(a, b, (((1,), (1,)), ((), ())), preferred_element_type=F32)


def _silu(x):
    return x * (1.0 / (1.0 + jnp.exp(-x)))


def _params(sem):
    return pltpu.CompilerParams(dimension_semantics=sem, vmem_limit_bytes=VMEM_LIMIT)


def _in_proj_body(combine, *refs):
    if combine:
        x_ref, y0_ref, y1_ref, gate_ref, g_ref, w_ref, xo_ref, hg_ref, qkv_ref, u_ref = refs
        gate = gate_ref[...]
        x = x_ref[...] + gate[:, 0:1] * y0_ref[...] + gate[:, 1:2] * y1_ref[...]
        xo_ref[...] = x
    else:
        x_ref, g_ref, w_ref, hg_ref, qkv_ref, u_ref = refs
        x = x_ref[...]
    h = x * lax.rsqrt(jnp.mean(x * x, axis=-1, keepdims=True) + NORM_EPS) * g_ref[...]
    hb = h.astype(BF16)
    hg_ref[...] = _dot(hb, w_ref[:, :HG_COLS])
    qkv_ref[...] = _dot(hb, w_ref[:, HG_COLS:HG_COLS + QKV_COLS])
    u_ref[...] = _dot(hb, w_ref[:, HG_COLS + QKV_COLS:])


def _in_proj(x, norm_g, w_bf, moe=None):
    T = x.shape[0]
    tm = min(ROW_TILE, T)
    nt = T // tm
    row = lambda i: (i, 0)
    fixed = lambda i: (0, 0)
    in_specs = [pl.BlockSpec((tm, D_MODEL), row)]
    args = [x]
    out_shape = []
    out_specs = []
    if moe is not None:
        ys, gate = moe
        in_specs += [pl.BlockSpec((tm, D_MODEL), row),
                     pl.BlockSpec((tm, D_MODEL), lambda i: (i + nt, 0)),
                     pl.BlockSpec((tm, LANES), row)]
        args += [ys, ys, gate]
        out_shape.append(jax.ShapeDtypeStruct((T, D_MODEL), F32))
        out_specs.append(pl.BlockSpec((tm, D_MODEL), row))
    in_specs += [pl.BlockSpec((1, D_MODEL), fixed), pl.BlockSpec((D_MODEL, IN_WIDTH), fixed)]
    args += [norm_g.reshape(1, D_MODEL), w_bf]
    out_shape += [jax.ShapeDtypeStruct((T, HG_COLS), F32), jax.ShapeDtypeStruct((T, QKV_COLS), F32),
                  jax.ShapeDtypeStruct((T, S5_WIDTH), F32)]
    out_specs += [pl.BlockSpec((tm, HG_COLS), row), pl.BlockSpec((tm, QKV_COLS), row),
                  pl.BlockSpec((tm, S5_WIDTH), row)]
    return pl.pallas_call(
        functools.partial(_in_proj_body, moe is not None),
        grid=(nt,), in_specs=in_specs, out_specs=out_specs, out_shape=out_shape,
        compiler_params=_params(("parallel",)), name="in_proj")(*args)


def _log_forget_and_key(z, log_lb, log_1m_lb):
    log_sig = jnp.minimum(z, 0.0) - jnp.log1p(jnp.exp(-jnp.abs(z)))
    a = log_1m_lb + log_sig
    log_f = jnp.maximum(a, log_lb) + jnp.log1p(jnp.exp(-jnp.abs(a - log_lb)))
    return log_f, jnp.exp(a - z)


def _split3(x):
    hi = x.astype(BF16)
    r = x - hi.astype(F32)
    mid = r.astype(BF16)
    lo = (r - mid.astype(F32)).astype(BF16)
    return hi, mid, lo


def _hgrn_body(hg_ref, lb_ref, gn_ref, o_ref, of_s, ob_s, st_s, kv_s):
    S = hg_ref.shape[0]
    n_blk = S // HG_BLOCK
    n_sub = HG_BLOCK // HG_CHUNK
    row = lax.broadcasted_iota(jnp.int32, (HG_BLOCK, HG_BLOCK), 0)
    col = lax.broadcasted_iota(jnp.int32, (HG_BLOCK, HG_BLOCK), 1)
    same = (row // HG_CHUNK) == (col // HG_CHUNK)
    lane_head = lax.broadcasted_iota(jnp.int32, (1, HG_WIDTH), 1) // HEAD_DIM
    col_sub = lax.broadcasted_iota(jnp.int32, (HG_WIDTH, HG_BLOCK), 1) // HG_CHUNK
    same_bf = jnp.where(same, 1.0, 0.0).astype(BF16)

    st_s[...] = jnp.zeros(st_s.shape, F32)

    def block(d, j, out_s):
        r0 = pl.multiple_of(j * HG_BLOCK, HG_BLOCK)
        rows = pl.ds(r0, HG_BLOCK)
        tri = same & ((col <= row) if d == 0 else (col >= row))
        z = hg_ref[rows, (3 + d) * HG_WIDTH:(4 + d) * HG_WIDTH]
        lf, k = _log_forget_and_key(z, lb_ref[2 * d:2 * d + 1, :], lb_ref[2 * d + 1:2 * d + 2, :])
        a_mat = jnp.concatenate([jnp.where(tri, 1.0, 0.0).astype(BF16), same_bf], axis=0)
        hi, mid, lo = _split3(lf)
        acc = _dot(a_mat, hi) + _dot(a_mat, mid) + _dot(a_mat, lo)
        b = acc[:HG_BLOCK]
        tot = acc[HG_BLOCK:]
        q = _silu(hg_ref[rows, 0:HG_WIDTH])
        v = hg_ref[rows, HG_WIDTH:2 * HG_WIDTH]
        q_dec = q * jnp.exp(b)
        k_dec = (k * jnp.exp(-b)).astype(BF16)
        k_end = (k * jnp.exp(tot - b)).astype(BF16)
        dec = jnp.exp(tot)
        q_heads = jnp.concatenate(
            [jnp.where(lane_head == h, q_dec, 0.0).astype(BF16) for h in range(HG_HEADS)], axis=0)
        scores = _dot_nt(q_heads, k_dec)
        o_blk = jnp.zeros((HG_BLOCK, HG_WIDTH), F32)
        for h in range(HG_HEADS):
            p_h = jnp.where(tri, scores[h * HG_BLOCK:(h + 1) * HG_BLOCK], 0.0).astype(BF16)
            o_blk += _dot(p_h, jnp.where(lane_head == h, v, 0.0).astype(BF16))
        v_t = v.T
        v_sub = jnp.concatenate(
            [jnp.where(col_sub == c, v_t, 0.0).astype(BF16) for c in range(n_sub)], axis=0)
        kv_s[...] = _dot(v_sub, k_end)
        o_inter = [None] * n_sub
        order = range(n_sub) if d == 0 else range(n_sub - 1, -1, -1)
        for c in order:
            q_c = jnp.concatenate(
                [q_heads[h * HG_BLOCK + c * HG_CHUNK:h * HG_BLOCK + (c + 1) * HG_CHUNK]
                 for h in range(HG_HEADS)], axis=0)
            st = st_s[d]
            o_all = _dot_nt(q_c, st.astype(BF16))
            o_c = jnp.zeros((HG_CHUNK, HG_WIDTH), F32)
            for h in range(HG_HEADS):
                o_c += jnp.where(lane_head == h, o_all[h * HG_CHUNK:(h + 1) * HG_CHUNK], 0.0)
            o_inter[c] = o_c
            st_s[d] = dec[c * HG_CHUNK:c * HG_CHUNK + 1, :] * st + kv_s[c * HG_WIDTH:(c + 1) * HG_WIDTH, :]
        out_s[rows, :] = o_blk + jnp.concatenate(o_inter, axis=0)

    def step(j, carry):
        block(0, j, of_s)
        block(1, n_blk - 1 - j, ob_s)
        return carry

    lax.fori_loop(0, n_blk, step, 0)

    r2 = lax.broadcasted_iota(jnp.int32, (HG_WIDTH, HG_WIDTH), 0) // HEAD_DIM
    c2 = lax.broadcasted_iota(jnp.int32, (HG_WIDTH, HG_WIDTH), 1) // HEAD_DIM
    head_mean = jnp.where(r2 == c2, 1.0 / HEAD_DIM, 0.0).astype(F32)

    def finish(j, carry):
        rows = pl.ds(pl.multiple_of(j * HG_BLOCK, HG_BLOCK), HG_BLOCK)
        o = of_s[rows, :] + ob_s[rows, :]
        ms = _dot_hi(o * o, head_mean)
        y = o * lax.rsqrt(ms + NORM_EPS) * gn_ref[...]
        o_ref[rows, :] = y * _silu(hg_ref[rows, 2 * HG_WIDTH:3 * HG_WIDTH])
        return carry

    lax.fori_loop(0, n_blk, finish, 0)


def _hgrn(hg, lb_rows, gn, B, S):
    return pl.pallas_call(
        _hgrn_body,
        grid=(B,),
        in_specs=[pl.BlockSpec((S, HG_COLS), lambda b: (b, 0)),
                  pl.BlockSpec((4, HG_WIDTH), lambda b: (0, 0)),
                  pl.BlockSpec((1, HG_WIDTH), lambda b: (0, 0))],
        out_specs=pl.BlockSpec((S, HG_WIDTH), lambda b: (b, 0)),
        out_shape=jax.ShapeDtypeStruct((B * S, HG_WIDTH), F32),
        scratch_shapes=[pltpu.VMEM((S, HG_WIDTH), F32), pltpu.VMEM((S, HG_WIDTH), F32),
                        pltpu.VMEM((2, HG_WIDTH, HG_WIDTH), F32),
                        pltpu.VMEM((HG_BLOCK // HG_CHUNK * HG_WIDTH, HG_WIDTH), F32)],
        compiler_params=_params(("parallel",)), name="hgrn2")(hg, lb_rows, gn)


def _attn_body(qkv_ref, cs_ref, gain_ref, o_ref, q_s, k_s, vlo_s, vhi_s):
    S = qkv_ref.shape[0]
    n_qb = S // ATTN_QB
    lane = lax.broadcasted_iota(jnp.int32, (1, LANES), 1)
    lo = lane < HEAD_DIM
    even = (lane % 2) == 0
    r2 = lax.broadcasted_iota(jnp.int32, (LANES, LANES), 0) // HEAD_DIM
    c2 = lax.broadcasted_iota(jnp.int32, (LANES, LANES), 1) // HEAD_DIM
    head_mean = jnp.where(r2 == c2, 1.0 / HEAD_DIM, 0.0).astype(F32)
    n_pairs = ATTN_WIDTH // LANES

    def norm_rope(x, gain, cos, sin):
        ms = _dot_hi(x * x, head_mean)
        y = x * lax.rsqrt(ms + NORM_EPS) * gain
        swapped = jnp.where(even, pltpu.roll(y, LANES - 1, 1), pltpu.roll(y, 1, 1))
        return y * cos + swapped * sin

    def prep(j, carry):
        rows = pl.ds(pl.multiple_of(j * ATTN_QB, ATTN_QB), ATTN_QB)
        cos = cs_ref[0, rows, :]
        sin = cs_ref[1, rows, :]
        for m in range(n_pairs):
            qm = norm_rope(qkv_ref[rows, m * LANES:(m + 1) * LANES], gain_ref[0:1, :], cos, sin)
            qm = qm * (HEAD_DIM ** -0.5)
            q_s[m, 0, rows, :] = jnp.where(lo, qm, 0.0).astype(BF16)
            q_s[m, 1, rows, :] = jnp.where(lo, 0.0, qm).astype(BF16)
        kk = norm_rope(qkv_ref[rows, ATTN_WIDTH:ATTN_WIDTH + LANES], gain_ref[1:2, :], cos, sin)
        kr = pltpu.roll(kk, HEAD_DIM, 1)
        k_s[0, rows, :] = jnp.where(lo, kk, kr).astype(BF16)
        k_s[1, rows, :] = jnp.where(lo, kr, kk).astype(BF16)
        vv = qkv_ref[rows, ATTN_WIDTH + LANES:ATTN_WIDTH + 2 * LANES]
        vr = pltpu.roll(vv, HEAD_DIM, 1)
        vlo_s[0, rows, :] = jnp.where(lo, vv, 0.0).astype(BF16)
        vhi_s[0, rows, :] = jnp.where(lo, 0.0, vr).astype(BF16)
        vlo_s[1, rows, :] = jnp.where(lo, vr, 0.0).astype(BF16)
        vhi_s[1, rows, :] = jnp.where(lo, 0.0, vv).astype(BF16)
        return carry

    lax.fori_loop(0, n_qb, prep, 0)

    def attend(j, carry):
        rows = pl.ds(pl.multiple_of(j * ATTN_QB, ATTN_QB), ATTN_QB)
        for m in range(n_pairs):
            kv = m // (n_pairs // 2)
            q2 = jnp.concatenate([q_s[m, 0, rows, :], q_s[m, 1, rows, :]], axis=0)
            s = _dot_nt(q2, k_s[kv])
            p = jnp.exp(s - jnp.max(s, axis=-1, keepdims=True))
            l = jnp.sum(p, axis=-1, keepdims=True)
            pb = p.astype(BF16)
            o = _dot(pb[:ATTN_QB], vlo_s[kv]) + _dot(pb[ATTN_QB:], vhi_s[kv])
            inv = jnp.where(lo, 1.0 / l[:ATTN_QB], 1.0 / l[ATTN_QB:])
            o_ref[rows, m * LANES:(m + 1) * LANES] = o * inv
        return carry

    lax.fori_loop(0, n_qb, attend, 0)


def _attention(qkv, cs, gains, B, S):
    n_pairs = ATTN_WIDTH // LANES
    return pl.pallas_call(
        _attn_body,
        grid=(B,),
        in_specs=[pl.BlockSpec((S, QKV_COLS), lambda b: (b, 0)),
                  pl.BlockSpec((2, S, LANES), lambda b: (0, 0, 0)),
                  pl.BlockSpec((2, LANES), lambda b: (0, 0))],
        out_specs=pl.BlockSpec((S, ATTN_WIDTH), lambda b: (b, 0)),
        out_shape=jax.ShapeDtypeStruct((B * S, ATTN_WIDTH), F32),
        scratch_shapes=[pltpu.VMEM((n_pairs, 2, S, LANES), BF16), pltpu.VMEM((2, S, LANES), BF16),
                        pltpu.VMEM((2, S, LANES), BF16), pltpu.VMEM((2, S, LANES), BF16)],
        compiler_params=_params(("parallel",)), name="gqa_attention")(qkv, cs, gains)


def _rope_tables(S):
    rows_count = S // GRID_W
    rows = jnp.repeat(jnp.arange(rows_count, dtype=F32), GRID_W)
    cols = jnp.tile(jnp.arange(GRID_W, dtype=F32), rows_count)
    pairs = HEAD_DIM // 4
    freqs = jnp.power(jnp.float32(ROPE_THETA), -jnp.arange(pairs, dtype=F32) / pairs)
    ang = jnp.concatenate([rows[:, None] * freqs, cols[:, None] * freqs], axis=-1)
    cos = jnp.repeat(jnp.cos(ang), 2, axis=-1)
    sin = jnp.repeat(jnp.sin(ang), 2, axis=-1) * jnp.tile(jnp.array([-1.0, 1.0], F32), HEAD_DIM // 2)
    reps = LANES // HEAD_DIM
    return jnp.stack([jnp.tile(cos, (1, reps)), jnp.tile(sin, (1, reps))])


def _s5_body(uf_ref, ub_ref, bmat_ref, lam_ref, cmat_ref, yf_ref, yb_ref, st_s, x_s):
    B, lt, _ = uf_ref.shape
    half = S5_REAL // 2

    @pl.when(pl.program_id(0) == 0)
    def _():
        st_s[...] = jnp.zeros(st_s.shape, F32)

    n_tiles = half // LANES

    def direction(d, u_ref, y_ref):
        u = u_ref[...].reshape(B * lt, S5_WIDTH).astype(BF16)
        bu = _dot(u, bmat_ref[d])
        for c in range(2 * n_tiles):
            x_s[c] = bu[:, c * LANES:(c + 1) * LANES]
        order = range(lt) if d == 0 else range(lt - 1, -1, -1)
        re_t = lambda c: slice(c * LANES, (c + 1) * LANES)
        im_t = lambda c: slice(half + c * LANES, half + (c + 1) * LANES)
        xr = [st_s[d, :, re_t(c)] for c in range(n_tiles)]
        xi = [st_s[d, :, im_t(c)] for c in range(n_tiles)]
        for t in order:
            slab = pl.ds(t, B, stride=lt)
            for c in range(n_tiles):
                lr = lam_ref[d, :, re_t(c)]
                li = lam_ref[d, :, im_t(c)]
                nr = lr * xr[c] - li * xi[c] + x_s[c, slab, :]
                ni = lr * xi[c] + li * xr[c] + x_s[n_tiles + c, slab, :]
                xr[c], xi[c] = nr, ni
                x_s[c, slab, :] = nr
                x_s[n_tiles + c, slab, :] = ni
        for c in range(n_tiles):
            st_s[d, :, re_t(c)] = xr[c]
            st_s[d, :, im_t(c)] = xi[c]
        xs = jnp.concatenate([x_s[c].astype(BF16) for c in range(2 * n_tiles)], axis=1)
        y_ref[...] = _dot(xs, cmat_ref[d]).reshape(B, lt, S5_WIDTH)

    direction(0, uf_ref, yf_ref)
    direction(1, ub_ref, yb_ref)


def _s5_scan(u, bmat, lam_b, cmat, B, S):
    lt = min(S5_LT, S)
    n = S // lt
    fwd = lambda j: (0, j, 0)
    bwd = lambda j: (0, n - 1 - j, 0)
    fixed = lambda j: (0, 0, 0)
    return pl.pallas_call(
        _s5_body,
        grid=(n,),
        in_specs=[pl.BlockSpec((B, lt, S5_WIDTH), fwd), pl.BlockSpec((B, lt, S5_WIDTH), bwd),
                  pl.BlockSpec((2, S5_WIDTH, S5_REAL), fixed), pl.BlockSpec((2, B, S5_REAL), fixed),
                  pl.BlockSpec((2, S5_REAL, S5_WIDTH), fixed)],
        out_specs=[pl.BlockSpec((B, lt, S5_WIDTH), fwd), pl.BlockSpec((B, lt, S5_WIDTH), bwd)],
        out_shape=[jax.ShapeDtypeStruct((B, S, S5_WIDTH), F32)] * 2,
        scratch_shapes=[pltpu.VMEM((2, B, S5_REAL), F32), pltpu.VMEM((S5_REAL // LANES, B * lt, LANES), F32)],
        compiler_params=_params(("arbitrary",)), name="s5_scan")(u, u, bmat, lam_b, cmat)


def _s5_operators(a_re, a_im, log_dt, b_re, b_im, c_re, c_im, B):
    dt = jnp.exp(log_dt.astype(F32))[..., None]
    ar, ai = a_re.astype(F32), a_im.astype(F32)
    mag = jnp.exp(ar * dt)
    lr, li = mag * jnp.cos(ai * dt), mag * jnp.sin(ai * dt)
    den = ar * ar + ai * ai
    fr = ((lr - 1.0) * ar + li * ai) / den
    fi = (li * ar - (lr - 1.0) * ai) / den
    br, bi = b_re.astype(F32), b_im.astype(F32)
    bbr = fr[..., None] * br - fi[..., None] * bi
    bbi = fr[..., None] * bi + fi[..., None] * br
    eye = jnp.eye(S5_GROUPS, dtype=F32)
    in_blk = lambda m: jnp.einsum('zgpc,gh->zgchp', m, eye).reshape(2, S5_WIDTH, S5_REAL // 2)
    out_blk = lambda m: jnp.einsum('zgcp,gh->zgphc', m, eye).reshape(2, S5_REAL // 2, S5_WIDTH)
    bmat = jnp.concatenate([in_blk(bbr), in_blk(bbi)], axis=-1).astype(BF16)
    cmat = jnp.concatenate([out_blk(c_re.astype(F32)), -out_blk(c_im.astype(F32))], axis=1).astype(BF16)
    lam = jnp.concatenate([lr.reshape(2, 1, -1), li.reshape(2, 1, -1)], axis=-1)
    return bmat, jnp.broadcast_to(lam, (2, B, S5_REAL)), cmat


def _gelu_tanh(x):
    return 0.5 * x * (1.0 + jnp.tanh(math.sqrt(2.0 / math.pi) * (x + 0.044715 * (x * x * x))))


def _out_proj_body(x_ref, oa_ref, ob_ref, u_ref, yf_ref, yb_ref, dskip_ref, wglu_ref, bglu_ref, wout_ref,
                   g2_ref, wr_ref, br_ref, x1_ref, h2_ref, eid_ref, gate_ref):
    y = dskip_ref[...] * u_ref[...] + yf_ref[...] + yb_ref[...]
    z = _gelu_tanh(y)
    t = _dot(z.astype(BF16), wglu_ref[...]) + bglu_ref[...]
    oc = z * (1.0 / (1.0 + jnp.exp(-t)))
    mix = _dot(oa_ref[...].astype(BF16), wout_ref[0:HG_WIDTH, :])
    mix += _dot(ob_ref[...].astype(BF16), wout_ref[HG_WIDTH:HG_WIDTH + ATTN_WIDTH, :])
    mix += _dot(oc.astype(BF16), wout_ref[HG_WIDTH + ATTN_WIDTH:, :])
    x1 = x_ref[...] + mix
    x1_ref[...] = x1
    h2 = x1 * lax.rsqrt(jnp.mean(x1 * x1, axis=-1, keepdims=True) + NORM_EPS) * g2_ref[...]
    h2_ref[...] = h2
    logits = _dot_hi(h2, wr_ref[...]) + br_ref[...]
    lane = lax.broadcasted_iota(jnp.int32, logits.shape, 1)
    lane_f = lane.astype(F32)
    neg = jnp.float32(-jnp.inf)
    is_g = lane < N_GROUPS
    gl = jnp.where(is_g, logits, neg)
    gmax = jnp.max(gl, axis=-1, keepdims=True)
    gsel = jnp.min(jnp.where(gl == gmax, lane_f, float(ROUTER_LANES)), axis=-1, keepdims=True)
    p_group = 1.0 / jnp.sum(jnp.where(is_g, jnp.exp(logits - gmax), 0.0), axis=-1, keepdims=True)
    first = N_GROUPS + PER_GROUP * gsel
    el = jnp.where((lane_f >= first) & (lane_f < first + PER_GROUP), logits, neg)
    v1 = jnp.max(el, axis=-1, keepdims=True)
    i1 = jnp.min(jnp.where(el == v1, lane_f, float(ROUTER_LANES)), axis=-1, keepdims=True)
    el2 = jnp.where(lane_f == i1, neg, el)
    v2 = jnp.max(el2, axis=-1, keepdims=True)
    i2 = jnp.min(jnp.where(el2 == v2, lane_f, float(ROUTER_LANES)), axis=-1, keepdims=True)
    e2 = jnp.exp(v2 - v1)
    w1 = p_group / (1.0 + e2)
    w2 = p_group * e2 / (1.0 + e2)
    eid_ref[...] = jnp.where(lane == 0, i1 - N_GROUPS, jnp.where(lane == 1, i2 - N_GROUPS, 0.0)).astype(jnp.int32)
    gate_ref[...] = jnp.where(lane == 0, w1, jnp.where(lane == 1, w2, 0.0))


def _out_proj(x, oa, ob, u, yf, yb, dskip, wglu_bf, bglu, wout_bf, g2, wr, br):
    T = x.shape[0]
    tm = min(ROW_TILE, T)
    row = lambda i: (i, 0)
    fixed = lambda i: (0, 0)
    blk = lambda w: pl.BlockSpec((tm, w), row)
    full = lambda a: pl.BlockSpec(a.shape, fixed)
    consts = [dskip, wglu_bf, bglu, wout_bf, g2, wr, br]
    return pl.pallas_call(
        _out_proj_body,
        grid=(T // tm,),
        in_specs=[blk(D_MODEL), blk(HG_WIDTH), blk(ATTN_WIDTH), blk(S5_WIDTH), blk(S5_WIDTH), blk(S5_WIDTH)]
        + [full(a) for a in consts],
        out_specs=[blk(D_MODEL), blk(D_MODEL), blk(ROUTER_LANES), blk(ROUTER_LANES)],
        out_shape=[jax.ShapeDtypeStruct((T, D_MODEL), F32), jax.ShapeDtypeStruct((T, D_MODEL), F32),
                   jax.ShapeDtypeStruct((T, ROUTER_LANES), jnp.int32),
                   jax.ShapeDtypeStruct((T, ROUTER_LANES), F32)],
        compiler_params=_params(("parallel",)), name="out_proj_router")(x, oa, ob, u, yf, yb, *consts)


def _gather_body(rows_per_step, idx_ref, src_ref, dst_ref, sem):
    base = pl.program_id(0) * rows_per_step

    def issue(r, carry):
        pltpu.make_async_copy(src_ref.at[pl.ds(idx_ref[0, 0, r], 1)], dst_ref.at[pl.ds(base + r, 1)], sem).start()
        return carry

    lax.fori_loop(0, rows_per_step, issue, 0)
    pltpu.make_async_copy(src_ref.at[pl.ds(0, rows_per_step)], dst_ref.at[pl.ds(base, rows_per_step)], sem).wait()


def _gather_rows(src, idx):
    n = idx.shape[0]
    g = min(GATHER_ROWS, n)
    steps = n // g
    return pl.pallas_call(
        functools.partial(_gather_body, g),
        grid=(steps,),
        in_specs=[pl.BlockSpec((1, 1, g), lambda i: (i, 0, 0), memory_space=pltpu.SMEM),
                  pl.BlockSpec(memory_space=pl.ANY)],
        out_specs=pl.BlockSpec(memory_space=pl.ANY),
        out_shape=jax.ShapeDtypeStruct((n, src.shape[1]), src.dtype),
        scratch_shapes=[pltpu.SemaphoreType.DMA(())],
        compiler_params=pltpu.CompilerParams(dimension_semantics=("arbitrary",)),
        name="row_gather")(idx.reshape(steps, 1, g), src)


def _moe_body(blk_e_ref, n_used_ref, xs_ref, wg_ref, wu_ref, wd_ref, o_ref):
    i = pl.program_id(0)

    @pl.when(i < n_used_ref[0])
    def _():
        xb = xs_ref[...].astype(BF16)
        g = _dot(xb, wg_ref[...])
        u = _dot(xb, wu_ref[...])
        o_ref[...] = _dot((_silu(g) * u).astype(BF16), wd_ref[...])

    @pl.when(i >= n_used_ref[0])
    def _():
        o_ref[...] = jnp.zeros(o_ref.shape, F32)


def _moe_experts(xs, blk_e, n_used, wg_bf, wu_bf, wd_bf, layer, bm):
    n_rows = xs.shape[0]
    return pl.pallas_call(
        _moe_body,
        grid_spec=pltpu.PrefetchScalarGridSpec(
            num_scalar_prefetch=2, grid=(n_rows // bm,),
            in_specs=[pl.BlockSpec((bm, D_MODEL), lambda i, be, nu: (i, 0)),
                      pl.BlockSpec((None, None, D_MODEL, EXPERT_FF), lambda i, be, nu: (layer, be[i], 0, 0)),
                      pl.BlockSpec((None, None, D_MODEL, EXPERT_FF), lambda i, be, nu: (layer, be[i], 0, 0)),
                      pl.BlockSpec((None, None, EXPERT_FF, D_MODEL), lambda i, be, nu: (layer, be[i], 0, 0))],
            out_specs=pl.BlockSpec((bm, D_MODEL), lambda i, be, nu: (i, 0))),
        out_shape=jax.ShapeDtypeStruct((n_rows, D_MODEL), F32),
        compiler_params=_params(("arbitrary",)), name="moe_experts")(blk_e, n_used, xs, wg_bf, wu_bf, wd_bf)


def _dispatch_tables(eid, T, bm):
    n_assign = T * TOP_K
    flat_e = eid.reshape(n_assign)
    onehot = (flat_e[:, None] == jnp.arange(N_EXPERTS, dtype=jnp.int32)[None, :]).astype(jnp.int32)
    csum = jnp.cumsum(onehot, axis=0)
    rank = jnp.sum(csum * onehot, axis=1) - 1
    counts = csum[-1]
    padded = ((counts + bm - 1) // bm) * bm
    pends = jnp.cumsum(padded)
    pstarts = pends - padded
    dest = (jnp.sum(onehot * pstarts[None, :], axis=1) + rank).astype(jnp.int32)
    n_blocks = n_assign // bm + N_EXPERTS
    tok = jnp.arange(n_assign, dtype=jnp.int32) // TOP_K
    row_tok = jnp.zeros((n_blocks * bm,), jnp.int32).at[dest].set(tok)
    blk_start = jnp.arange(n_blocks, dtype=jnp.int32) * bm
    blk_e = jnp.minimum(jnp.searchsorted(pends, blk_start, side='right'), N_EXPERTS - 1).astype(jnp.int32)
    n_used = (pends[-1] // bm).astype(jnp.int32).reshape(1)
    return row_tok, blk_e, n_used, dest


def _moe(h2, eid, wg_bf, wu_bf, wd_bf, layer):
    T = h2.shape[0]
    bm = min(MOE_BM, T)
    row_tok, blk_e, n_used, dest = _dispatch_tables(eid, T, bm)
    xs = _gather_rows(h2, row_tok)
    ys = _moe_experts(xs, blk_e, n_used, wg_bf, wu_bf, wd_bf, layer, bm)
    return _gather_rows(ys, dest.reshape(T, TOP_K).T.reshape(-1))


def _final_body(x_ref, y0_ref, y1_ref, gate_ref, g_ref, o_ref):
    gate = gate_ref[...]
    x = x_ref[...] + gate[:, 0:1] * y0_ref[...] + gate[:, 1:2] * y1_ref[...]
    o_ref[...] = x * lax.rsqrt(jnp.mean(x * x, axis=-1, keepdims=True) + NORM_EPS) * g_ref[...]


def _final_norm(x, ys, gate, g):
    T = x.shape[0]
    tm = min(ROW_TILE, T)
    nt = T // tm
    row = lambda i: (i, 0)
    return pl.pallas_call(
        _final_body,
        grid=(nt,),
        in_specs=[pl.BlockSpec((tm, D_MODEL), row), pl.BlockSpec((tm, D_MODEL), row),
                  pl.BlockSpec((tm, D_MODEL), lambda i: (i + nt, 0)), pl.BlockSpec((tm, LANES), row),
                  pl.BlockSpec((1, D_MODEL), lambda i: (0, 0))],
        out_specs=pl.BlockSpec((tm, D_MODEL), row),
        out_shape=jax.ShapeDtypeStruct((T, D_MODEL), F32),
        compiler_params=_params(("parallel",)), name="combine_final_norm")(x, ys, ys, gate, g.reshape(1, D_MODEL))


def kernel(x, norm1_g, w_in, hgrn_lower_bounds, hgrn_norm_g, attn_q_norm_g, attn_k_norm_g, s5_a_re, s5_a_im, s5_log_dt, s5_b_re, s5_b_im, s5_c_re, s5_c_im, s5_d, s5_w_glu, s5_b_glu, w_out, norm2_g, router_group_w, router_group_b, router_expert_w, router_expert_b, expert_w_gate, expert_w_up, expert_w_down, final_norm_g):
    B, S, D = x.shape
    T = B * S
    depth = w_in.shape[0]
    cs = _rope_tables(S)
    lb_all = jnp.cumsum(jax.nn.softmax(hgrn_lower_bounds.astype(F32), axis=0), axis=0)
    lb_all = lb_all - lb_all[0:1]
    wg_bf, wu_bf, wd_bf = expert_w_gate.astype(BF16), expert_w_up.astype(BF16), expert_w_down.astype(BF16)
    reps = LANES // HEAD_DIM
    pad = ROUTER_LANES - N_GROUPS - N_EXPERTS

    xt = x.reshape(T, D)
    moe = None
    for l in range(depth):
        if moe is None:
            hg, qkv, u = _in_proj(xt, norm1_g[l], w_in[l].astype(BF16))
        else:
            xt, hg, qkv, u = _in_proj(xt, norm1_g[l], w_in[l].astype(BF16), moe)
        lb = lb_all[l]
        lb_rows = jnp.stack([jnp.log(lb[0]), jnp.log1p(-lb[0]), jnp.log(lb[1]), jnp.log1p(-lb[1])])
        oa = _hgrn(hg, lb_rows, jnp.tile(hgrn_norm_g[l].astype(F32), HG_HEADS).reshape(1, HG_WIDTH), B, S)
        gains = jnp.stack([jnp.tile(attn_q_norm_g[l].astype(F32), reps), jnp.tile(attn_k_norm_g[l].astype(F32), reps)])
        ob = _attention(qkv, cs, gains, B, S)
        bmat, lam_b, cmat = _s5_operators(s5_a_re[l], s5_a_im[l], s5_log_dt[l], s5_b_re[l], s5_b_im[l],
                                          s5_c_re[l], s5_c_im[l], B)
        yf, yb = _s5_scan(u.reshape(B, S, S5_WIDTH), bmat, lam_b, cmat, B, S)
        wr = jnp.concatenate([router_group_w[l].astype(F32), router_expert_w[l].astype(F32),
                              jnp.zeros((D, pad), F32)], axis=1)
        br = jnp.concatenate([router_group_b[l].astype(F32), router_expert_b[l].astype(F32),
                              jnp.zeros((pad,), F32)]).reshape(1, ROUTER_LANES)
        xt, h2, eid, gate = _out_proj(
            xt, oa, ob, u, yf.reshape(T, S5_WIDTH), yb.reshape(T, S5_WIDTH),
            s5_d[l].astype(F32).reshape(1, S5_WIDTH), s5_w_glu[l].astype(BF16),
            s5_b_glu[l].astype(F32).reshape(1, S5_WIDTH), w_out[l].astype(BF16),
            norm2_g[l].astype(F32).reshape(1, D), wr, br)
        moe = (_moe(h2, eid[:, :TOP_K], wg_bf, wu_bf, wd_bf, l), gate)
    out = _final_norm(xt, moe[0], moe[1], final_norm_g.astype(F32))
    return out.reshape(B, S, D)
```

```python
import functools
import math

import jax
import jax.numpy as jnp
from jax import lax
from jax.experimental import pallas as pl
from jax.experimental.pallas import tpu as pltpu

F32 = jnp.float32
BF16 = jnp.bfloat16
HIGHEST = lax.Precision.HIGHEST

D_MODEL = 1024
HEAD_DIM = 64
HG_WIDTH = 256
S5_WIDTH = 256
ATTN_WIDTH = 512
KV_WIDTH = 128
HG_HEADS = HG_WIDTH // HEAD_DIM
HG_CHUNK = 16
ROPE_THETA = 10000.0
GRID_W = 64
S5_CH = 16
S5_GROUPS = S5_WIDTH // S5_CH
S5_STATE = 64
N_GROUPS = 4
PER_GROUP = 8
N_EXPERTS = N_GROUPS * PER_GROUP
TOP_K = 2
EXPERT_FF = D_MODEL // 2
NORM_EPS = 1e-6
HG_COLS = 5 * HG_WIDTH
QKV_COLS = ATTN_WIDTH + 2 * KV_WIDTH
IN_WIDTH = HG_COLS + QKV_COLS + S5_WIDTH
S5_REAL = 2 * S5_GROUPS * S5_STATE

LANES = 128
HG_BLOCK = 128
ATTN_QB = 128
S5_LT = 32
ROW_TILE = 512
MOE_BM = 256
GATHER_ROWS = 512
ROUTER_LANES = 128
VMEM_LIMIT = 56 * 1024 * 1024


def _dot(a, b):
    return jnp.dot(a, b, preferred_element_type=F32)


def _dot_hi(a, b):
    return jnp.dot(a, b, preferred_element_type=F32, precision=HIGHEST)


def _dot_nt(a, b):
    return lax.dot_general(a, b, (((1,), (1,)), ((), ())), preferred_element_type=F32)


def _silu(x):
    return x * (1.0 / (1.0 + jnp.exp(-x)))


def _params(sem):
    return pltpu.CompilerParams(dimension_semantics=sem, vmem_limit_bytes=VMEM_LIMIT)


ROW_SUB = D_MODEL // LANES


def _load_rows(ref, n):
    return jnp.concatenate([ref[pl.ds(j, n, stride=ROW_SUB), :] for j in range(ROW_SUB)], axis=1)


def _store_rows(ref, val):
    n = val.shape[0]
    for j in range(ROW_SUB):
        ref[pl.ds(j, n, stride=ROW_SUB), :] = val[:, j * LANES:(j + 1) * LANES]


def _in_proj_body(combine, *refs):
    if combine:
        x_ref, y0_ref, y1_ref, gate_ref, g_ref, w_ref, xo_ref, hg_ref, qkv_ref, u_ref = refs
        gate = gate_ref[...]
        n = x_ref.shape[0]
        x = x_ref[...] + gate[:, 0:1] * _load_rows(y0_ref, n) + gate[:, 1:2] * _load_rows(y1_ref, n)
        xo_ref[...] = x
    else:
        x_ref, g_ref, w_ref, hg_ref, qkv_ref, u_ref = refs
        x = x_ref[...]
    h = x * lax.rsqrt(jnp.mean(x * x, axis=-1, keepdims=True) + NORM_EPS) * g_ref[...]
    hb = h.astype(BF16)
    hg_ref[...] = _dot(hb, w_ref[:, :HG_COLS])
    qkv_ref[...] = _dot(hb, w_ref[:, HG_COLS:HG_COLS + QKV_COLS])
    u_ref[...] = _dot(hb, w_ref[:, HG_COLS + QKV_COLS:])


def _in_proj(x, norm_g, w_bf, moe=None):
    T = x.shape[0]
    tm = min(ROW_TILE, T)
    nt = T // tm
    row = lambda i: (i, 0)
    fixed = lambda i: (0, 0)
    in_specs = [pl.BlockSpec((tm, D_MODEL), row)]
    args = [x]
    out_shape = []
    out_specs = []
    if moe is not None:
        ys, gate = moe
        in_specs += [pl.BlockSpec((tm * ROW_SUB, LANES), row),
                     pl.BlockSpec((tm * ROW_SUB, LANES), lambda i: (i + nt, 0)),
                     pl.BlockSpec((tm, LANES), row)]
        args += [ys, ys, gate]
        out_shape.append(jax.ShapeDtypeStruct((T, D_MODEL), F32))
        out_specs.append(pl.BlockSpec((tm, D_MODEL), row))
    in_specs += [pl.BlockSpec((1, D_MODEL), fixed), pl.BlockSpec((D_MODEL, IN_WIDTH), fixed)]
    args += [norm_g.reshape(1, D_MODEL), w_bf]
    out_shape += [jax.ShapeDtypeStruct((T, HG_COLS), F32), jax.ShapeDtypeStruct((T, QKV_COLS), F32),
                  jax.ShapeDtypeStruct((T, S5_WIDTH), F32)]
    out_specs += [pl.BlockSpec((tm, HG_COLS), row), pl.BlockSpec((tm, QKV_COLS), row),
                  pl.BlockSpec((tm, S5_WIDTH), row)]
    return pl.pallas_call(
        functools.partial(_in_proj_body, moe is not None),
        grid=(nt,), in_specs=in_specs, out_specs=out_specs, out_shape=out_shape,
        compiler_params=_params(("parallel",)), name="in_proj")(*args)


def _log_forget_and_key(z, log_lb, log_1m_lb):
    log_sig = jnp.minimum(z, 0.0) - jnp.log1p(jnp.exp(-jnp.abs(z)))
    a = log_1m_lb + log_sig
    log_f = jnp.maximum(a, log_lb) + jnp.log1p(jnp.exp(-jnp.abs(a - log_lb)))
    return log_f, jnp.exp(a - z)


def _split3(x):
    hi = x.astype(BF16)
    r = x - hi.astype(F32)
    mid = r.astype(BF16)
    lo = (r - mid.astype(F32)).astype(BF16)
    return hi, mid, lo


def _hgrn_body(hg_ref, lb_ref, gn_ref, o_ref, of_s, ob_s, st_s, kv_s):
    S = hg_ref.shape[0]
    n_blk = S // HG_BLOCK
    n_sub = HG_BLOCK // HG_CHUNK
    row = lax.broadcasted_iota(jnp.int32, (HG_BLOCK, HG_BLOCK), 0)
    col = lax.broadcasted_iota(jnp.int32, (HG_BLOCK, HG_BLOCK), 1)
    same = (row // HG_CHUNK) == (col // HG_CHUNK)
    lane_head = lax.broadcasted_iota(jnp.int32, (1, HG_WIDTH), 1) // HEAD_DIM
    col_sub = lax.broadcasted_iota(jnp.int32, (HG_WIDTH, HG_BLOCK), 1) // HG_CHUNK
    same_bf = jnp.where(same, 1.0, 0.0).astype(BF16)

    st_s[...] = jnp.zeros(st_s.shape, F32)

    def block(d, j, out_s):
        r0 = pl.multiple_of(j * HG_BLOCK, HG_BLOCK)
        rows = pl.ds(r0, HG_BLOCK)
        tri = same & ((col <= row) if d == 0 else (col >= row))
        z = hg_ref[rows, (3 + d) * HG_WIDTH:(4 + d) * HG_WIDTH]
        lf, k = _log_forget_and_key(z, lb_ref[2 * d:2 * d + 1, :], lb_ref[2 * d + 1:2 * d + 2, :])
        a_mat = jnp.concatenate([jnp.where(tri, 1.0, 0.0).astype(BF16), same_bf], axis=0)
        hi, mid, lo = _split3(lf)
        acc = _dot(a_mat, hi) + _dot(a_mat, mid) + _dot(a_mat, lo)
        b = acc[:HG_BLOCK]
        tot = acc[HG_BLOCK:]
        q = _silu(hg_ref[rows, 0:HG_WIDTH])
        v = hg_ref[rows, HG_WIDTH:2 * HG_WIDTH]
        q_dec = q * jnp.exp(b)
        k_dec = (k * jnp.exp(-b)).astype(BF16)
        k_end = (k * jnp.exp(tot - b)).astype(BF16)
        dec = jnp.exp(tot)
        q_heads = jnp.concatenate(
            [jnp.where(lane_head == h, q_dec, 0.0).astype(BF16) for h in range(HG_HEADS)], axis=0)
        scores = _dot_nt(q_heads, k_dec)
        o_blk = jnp.zeros((HG_BLOCK, HG_WIDTH), F32)
        for h in range(HG_HEADS):
            p_h = jnp.where(tri, scores[h * HG_BLOCK:(h + 1) * HG_BLOCK], 0.0).astype(BF16)
            o_blk += _dot(p_h, jnp.where(lane_head == h, v, 0.0).astype(BF16))
        v_t = v.T
        v_sub = jnp.concatenate(
            [jnp.where(col_sub == c, v_t, 0.0).astype(BF16) for c in range(n_sub)], axis=0)
        kv_s[...] = _dot(v_sub, k_end)
        o_inter = [None] * n_sub
        order = range(n_sub) if d == 0 else range(n_sub - 1, -1, -1)
        for c in order:
            q_c = jnp.concatenate(
                [q_heads[h * HG_BLOCK + c * HG_CHUNK:h * HG_BLOCK + (c + 1) * HG_CHUNK]
                 for h in range(HG_HEADS)], axis=0)
            st = st_s[d]
            o_all = _dot_nt(q_c, st.astype(BF16))
            o_c = jnp.zeros((HG_CHUNK, HG_WIDTH), F32)
            for h in range(HG_HEADS):
                o_c += jnp.where(lane_head == h, o_all[h * HG_CHUNK:(h + 1) * HG_CHUNK], 0.0)
            o_inter[c] = o_c
            st_s[d] = dec[c * HG_CHUNK:c * HG_CHUNK + 1, :] * st + kv_s[c * HG_WIDTH:(c + 1) * HG_WIDTH, :]
        out_s[rows, :] = o_blk + jnp.concatenate(o_inter, axis=0)

    def step(j, carry):
        block(0, j, of_s)
        block(1, n_blk - 1 - j, ob_s)
        return carry

    lax.fori_loop(0, n_blk, step, 0)

    r2 = lax.broadcasted_iota(jnp.int32, (HG_WIDTH, HG_WIDTH), 0) // HEAD_DIM
    c2 = lax.broadcasted_iota(jnp.int32, (HG_WIDTH, HG_WIDTH), 1) // HEAD_DIM
    head_mean = jnp.where(r2 == c2, 1.0 / HEAD_DIM, 0.0).astype(F32)

    def finish(j, carry):
        rows = pl.ds(pl.multiple_of(j * HG_BLOCK, HG_BLOCK), HG_BLOCK)
        o = of_s[rows, :] + ob_s[rows, :]
        ms = _dot_hi(o * o, head_mean)
        y = o * lax.rsqrt(ms + NORM_EPS) * gn_ref[...]
        o_ref[rows, :] = y * _silu(hg_ref[rows, 2 * HG_WIDTH:3 * HG_WIDTH])
        return carry

    lax.fori_loop(0, n_blk, finish, 0)


def _hgrn(hg, lb_rows, gn, B, S):
    return pl.pallas_call(
        _hgrn_body,
        grid=(B,),
        in_specs=[pl.BlockSpec((S, HG_COLS), lambda b: (b, 0)),
                  pl.BlockSpec((4, HG_WIDTH), lambda b: (0, 0)),
                  pl.BlockSpec((1, HG_WIDTH), lambda b: (0, 0))],
        out_specs=pl.BlockSpec((S, HG_WIDTH), lambda b: (b, 0)),
        out_shape=jax.ShapeDtypeStruct((B * S, HG_WIDTH), F32),
        scratch_shapes=[pltpu.VMEM((S, HG_WIDTH), F32), pltpu.VMEM((S, HG_WIDTH), F32),
                        pltpu.VMEM((2, HG_WIDTH, HG_WIDTH), F32),
                        pltpu.VMEM((HG_BLOCK // HG_CHUNK * HG_WIDTH, HG_WIDTH), F32)],
        compiler_params=_params(("parallel",)), name="hgrn2")(hg, lb_rows, gn)


def _attn_body(qkv_ref, cs_ref, gain_ref, o_ref, q_s, k_s, vlo_s, vhi_s):
    S = qkv_ref.shape[0]
    n_qb = S // ATTN_QB
    lane = lax.broadcasted_iota(jnp.int32, (1, LANES), 1)
    lo = lane < HEAD_DIM
    even = (lane % 2) == 0
    r2 = lax.broadcasted_iota(jnp.int32, (LANES, LANES), 0) // HEAD_DIM
    c2 = lax.broadcasted_iota(jnp.int32, (LANES, LANES), 1) // HEAD_DIM
    head_mean = jnp.where(r2 == c2, 1.0 / HEAD_DIM, 0.0).astype(F32)
    n_pairs = ATTN_WIDTH // LANES

    def norm_rope(x, gain, cos, sin):
        ms = _dot_hi(x * x, head_mean)
        y = x * lax.rsqrt(ms + NORM_EPS) * gain
        swapped = jnp.where(even, pltpu.roll(y, LANES - 1, 1), pltpu.roll(y, 1, 1))
        return y * cos + swapped * sin

    def prep(j, carry):
        rows = pl.ds(pl.multiple_of(j * ATTN_QB, ATTN_QB), ATTN_QB)
        cos = cs_ref[0, rows, :]
        sin = cs_ref[1, rows, :]
        for m in range(n_pairs):
            qm = norm_rope(qkv_ref[rows, m * LANES:(m + 1) * LANES], gain_ref[0:1, :], cos, sin)
            qm = qm * (HEAD_DIM ** -0.5)
            q_s[m, 0, rows, :] = jnp.where(lo, qm, 0.0).astype(BF16)
            q_s[m, 1, rows, :] = jnp.where(lo, 0.0, qm).astype(BF16)
        kk = norm_rope(qkv_ref[rows, ATTN_WIDTH:ATTN_WIDTH + LANES], gain_ref[1:2, :], cos, sin)
        kr = pltpu.roll(kk, HEAD_DIM, 1)
        k_s[0, rows, :] = jnp.where(lo, kk, kr).astype(BF16)
        k_s[1, rows, :] = jnp.where(lo, kr, kk).astype(BF16)
        vv = qkv_ref[rows, ATTN_WIDTH + LANES:ATTN_WIDTH + 2 * LANES]
        vr = pltpu.roll(vv, HEAD_DIM, 1)
        vlo_s[0, rows, :] = jnp.where(lo, vv, 0.0).astype(BF16)
        vhi_s[0, rows, :] = jnp.where(lo, 0.0, vr).astype(BF16)
        vlo_s[1, rows, :] = jnp.where(lo, vr, 0.0).astype(BF16)
        vhi_s[1, rows, :] = jnp.where(lo, 0.0, vv).astype(BF16)
        return carry

    lax.fori_loop(0, n_qb, prep, 0)

    def attend(j, carry):
        rows = pl.ds(pl.multiple_of(j * ATTN_QB, ATTN_QB), ATTN_QB)
        for m in range(n_pairs):
            kv = m // (n_pairs // 2)
            q2 = jnp.concatenate([q_s[m, 0, rows, :], q_s[m, 1, rows, :]], axis=0)
            s = _dot_nt(q2, k_s[kv])
            p = jnp.exp(s - jnp.max(s, axis=-1, keepdims=True))
            l = jnp.sum(p, axis=-1, keepdims=True)
            pb = p.astype(BF16)
            o = _dot(pb[:ATTN_QB], vlo_s[kv]) + _dot(pb[ATTN_QB:], vhi_s[kv])
            inv = jnp.where(lo, 1.0 / l[:ATTN_QB], 1.0 / l[ATTN_QB:])
            o_ref[rows, m * LANES:(m + 1) * LANES] = o * inv
        return carry

    lax.fori_loop(0, n_qb, attend, 0)


def _attention(qkv, cs, gains, B, S):
    n_pairs = ATTN_WIDTH // LANES
    return pl.pallas_call(
        _attn_body,
        grid=(B,),
        in_specs=[pl.BlockSpec((S, QKV_COLS), lambda b: (b, 0)),
                  pl.BlockSpec((2, S, LANES), lambda b: (0, 0, 0)),
                  pl.BlockSpec((2, LANES), lambda b: (0, 0))],
        out_specs=pl.BlockSpec((S, ATTN_WIDTH), lambda b: (b, 0)),
        out_shape=jax.ShapeDtypeStruct((B * S, ATTN_WIDTH), F32),
        scratch_shapes=[pltpu.VMEM((n_pairs, 2, S, LANES), BF16), pltpu.VMEM((2, S, LANES), BF16),
                        pltpu.VMEM((2, S, LANES), BF16), pltpu.VMEM((2, S, LANES), BF16)],
        compiler_params=_params(("parallel",)), name="gqa_attention")(qkv, cs, gains)


def _rope_tables(S):
    rows_count = S // GRID_W
    rows = jnp.repeat(jnp.arange(rows_count, dtype=F32), GRID_W)
    cols = jnp.tile(jnp.arange(GRID_W, dtype=F32), rows_count)
    pairs = HEAD_DIM // 4
    freqs = jnp.power(jnp.float32(ROPE_THETA), -jnp.arange(pairs, dtype=F32) / pairs)
    ang = jnp.concatenate([rows[:, None] * freqs, cols[:, None] * freqs], axis=-1)
    cos = jnp.repeat(jnp.cos(ang), 2, axis=-1)
    sin = jnp.repeat(jnp.sin(ang), 2, axis=-1) * jnp.tile(jnp.array([-1.0, 1.0], F32), HEAD_DIM // 2)
    reps = LANES // HEAD_DIM
    return jnp.stack([jnp.tile(cos, (1, reps)), jnp.tile(sin, (1, reps))])


def _s5_body(uf_ref, ub_ref, bmat_ref, lam_ref, cmat_ref, yf_ref, yb_ref, st_s, x_s):
    B, lt, _ = uf_ref.shape
    half = S5_REAL // 2

    @pl.when(pl.program_id(0) == 0)
    def _():
        st_s[...] = jnp.zeros(st_s.shape, F32)

    n_tiles = half // LANES

    def direction(d, u_ref, y_ref):
        u = u_ref[...].reshape(B * lt, S5_WIDTH).astype(BF16)
        bu = _dot(u, bmat_ref[d])
        for c in range(2 * n_tiles):
            x_s[c] = bu[:, c * LANES:(c + 1) * LANES]
        order = range(lt) if d == 0 else range(lt - 1, -1, -1)
        re_t = lambda c: slice(c * LANES, (c + 1) * LANES)
        im_t = lambda c: slice(half + c * LANES, half + (c + 1) * LANES)
        xr = [st_s[d, :, re_t(c)] for c in range(n_tiles)]
        xi = [st_s[d, :, im_t(c)] for c in range(n_tiles)]
        for t in order:
            slab = pl.ds(t, B, stride=lt)
            for c in range(n_tiles):
                lr = lam_ref[d, :, re_t(c)]
                li = lam_ref[d, :, im_t(c)]
                nr = lr * xr[c] - li * xi[c] + x_s[c, slab, :]
                ni = lr * xi[c] + li * xr[c] + x_s[n_tiles + c, slab, :]
                xr[c], xi[c] = nr, ni
                x_s[c, slab, :] = nr
                x_s[n_tiles + c, slab, :] = ni
        for c in range(n_tiles):
            st_s[d, :, re_t(c)] = xr[c]
            st_s[d, :, im_t(c)] = xi[c]
        xs = jnp.concatenate([x_s[c].astype(BF16) for c in range(2 * n_tiles)], axis=1)
        y_ref[...] = _dot(xs, cmat_ref[d]).reshape(B, lt, S5_WIDTH)

    direction(0, uf_ref, yf_ref)
    direction(1, ub_ref, yb_ref)


def _s5_scan(u, bmat, lam_b, cmat, B, S):
    lt = min(S5_LT, S)
    n = S // lt
    fwd = lambda j: (0, j, 0)
    bwd = lambda j: (0, n - 1 - j, 0)
    fixed = lambda j: (0, 0, 0)
    return pl.pallas_call(
        _s5_body,
        grid=(n,),
        in_specs=[pl.BlockSpec((B, lt, S5_WIDTH), fwd), pl.BlockSpec((B, lt, S5_WIDTH), bwd),
                  pl.BlockSpec((2, S5_WIDTH, S5_REAL), fixed), pl.BlockSpec((2, B, S5_REAL), fixed),
                  pl.BlockSpec((2, S5_REAL, S5_WIDTH), fixed)],
        out_specs=[pl.BlockSpec((B, lt, S5_WIDTH), fwd), pl.BlockSpec((B, lt, S5_WIDTH), bwd)],
        out_shape=[jax.ShapeDtypeStruct((B, S, S5_WIDTH), F32)] * 2,
        scratch_shapes=[pltpu.VMEM((2, B, S5_REAL), F32), pltpu.VMEM((S5_REAL // LANES, B * lt, LANES), F32)],
        compiler_params=_params(("arbitrary",)), name="s5_scan")(u, u, bmat, lam_b, cmat)


def _s5_operators(a_re, a_im, log_dt, b_re, b_im, c_re, c_im, B):
    dt = jnp.exp(log_dt.astype(F32))[..., None]
    ar, ai = a_re.astype(F32), a_im.astype(F32)
    mag = jnp.exp(ar * dt)
    lr, li = mag * jnp.cos(ai * dt), mag * jnp.sin(ai * dt)
    den = ar * ar + ai * ai
    fr = ((lr - 1.0) * ar + li * ai) / den
    fi = (li * ar - (lr - 1.0) * ai) / den
    br, bi = b_re.astype(F32), b_im.astype(F32)
    bbr = fr[..., None] * br - fi[..., None] * bi
    bbi = fr[..., None] * bi + fi[..., None] * br
    eye = jnp.eye(S5_GROUPS, dtype=F32)
    in_blk = lambda m: jnp.einsum('zgpc,gh->zgchp', m, eye).reshape(2, S5_WIDTH, S5_REAL // 2)
    out_blk = lambda m: jnp.einsum('zgcp,gh->zgphc', m, eye).reshape(2, S5_REAL // 2, S5_WIDTH)
    bmat = jnp.concatenate([in_blk(bbr), in_blk(bbi)], axis=-1).astype(BF16)
    cmat = jnp.concatenate([out_blk(c_re.astype(F32)), -out_blk(c_im.astype(F32))], axis=1).astype(BF16)
    lam = jnp.concatenate([lr.reshape(2, 1, -1), li.reshape(2, 1, -1)], axis=-1)
    return bmat, jnp.broadcast_to(lam, (2, B, S5_REAL)), cmat


def _gelu_tanh(x):
    return 0.5 * x * (1.0 + jnp.tanh(math.sqrt(2.0 / math.pi) * (x + 0.044715 * (x * x * x))))


def _out_proj_body(x_ref, oa_ref, ob_ref, u_ref, yf_ref, yb_ref, dskip_ref, wglu_ref, bglu_ref, wout_ref,
                   g2_ref, wr_ref, br_ref, x1_ref, h2_ref, eid_ref, gate_ref):
    y = dskip_ref[...] * u_ref[...] + yf_ref[...] + yb_ref[...]
    z = _gelu_tanh(y)
    t = _dot(z.astype(BF16), wglu_ref[...]) + bglu_ref[...]
    oc = z * (1.0 / (1.0 + jnp.exp(-t)))
    mix = _dot(oa_ref[...].astype(BF16), wout_ref[0:HG_WIDTH, :])
    mix += _dot(ob_ref[...].astype(BF16), wout_ref[HG_WIDTH:HG_WIDTH + ATTN_WIDTH, :])
    mix += _dot(oc.astype(BF16), wout_ref[HG_WIDTH + ATTN_WIDTH:, :])
    x1 = x_ref[...] + mix
    x1_ref[...] = x1
    h2 = x1 * lax.rsqrt(jnp.mean(x1 * x1, axis=-1, keepdims=True) + NORM_EPS) * g2_ref[...]
    _store_rows(h2_ref, h2)
    logits = _dot_hi(h2, wr_ref[...]) + br_ref[...]
    lane = lax.broadcasted_iota(jnp.int32, logits.shape, 1)
    lane_f = lane.astype(F32)
    neg = jnp.float32(-jnp.inf)
    is_g = lane < N_GROUPS
    gl = jnp.where(is_g, logits, neg)
    gmax = jnp.max(gl, axis=-1, keepdims=True)
    gsel = jnp.min(jnp.where(gl == gmax, lane_f, float(ROUTER_LANES)), axis=-1, keepdims=True)
    p_group = 1.0 / jnp.sum(jnp.where(is_g, jnp.exp(logits - gmax), 0.0), axis=-1, keepdims=True)
    first = N_GROUPS + PER_GROUP * gsel
    el = jnp.where((lane_f >= first) & (lane_f < first + PER_GROUP), logits, neg)
    v1 = jnp.max(el, axis=-1, keepdims=True)
    i1 = jnp.min(jnp.where(el == v1, lane_f, float(ROUTER_LANES)), axis=-1, keepdims=True)
    el2 = jnp.where(lane_f == i1, neg, el)
    v2 = jnp.max(el2, axis=-1, keepdims=True)
    i2 = jnp.min(jnp.where(el2 == v2, lane_f, float(ROUTER_LANES)), axis=-1, keepdims=True)
    e2 = jnp.exp(v2 - v1)
    w1 = p_group / (1.0 + e2)
    w2 = p_group * e2 / (1.0 + e2)
    eid_ref[...] = jnp.where(lane == 0, i1 - N_GROUPS, jnp.where(lane == 1, i2 - N_GROUPS, 0.0)).astype(jnp.int32)
    gate_ref[...] = jnp.where(lane == 0, w1, jnp.where(lane == 1, w2, 0.0))


def _out_proj(x, oa, ob, u, yf, yb, dskip, wglu_bf, bglu, wout_bf, g2, wr, br):
    T = x.shape[0]
    tm = min(ROW_TILE, T)
    row = lambda i: (i, 0)
    fixed = lambda i: (0, 0)
    blk = lambda w: pl.BlockSpec((tm, w), row)
    full = lambda a: pl.BlockSpec(a.shape, fixed)
    consts = [dskip, wglu_bf, bglu, wout_bf, g2, wr, br]
    return pl.pallas_call(
        _out_proj_body,
        grid=(T // tm,),
        in_specs=[blk(D_MODEL), blk(HG_WIDTH), blk(ATTN_WIDTH), blk(S5_WIDTH), blk(S5_WIDTH), blk(S5_WIDTH)]
        + [full(a) for a in consts],
        out_specs=[blk(D_MODEL), pl.BlockSpec((tm * ROW_SUB, LANES), row), blk(ROUTER_LANES), blk(ROUTER_LANES)],
        out_shape=[jax.ShapeDtypeStruct((T, D_MODEL), F32), jax.ShapeDtypeStruct((T * ROW_SUB, LANES), F32),
                   jax.ShapeDtypeStruct((T, ROUTER_LANES), jnp.int32),
                   jax.ShapeDtypeStruct((T, ROUTER_LANES), F32)],
        compiler_params=_params(("parallel",)), name="out_proj_router")(x, oa, ob, u, yf, yb, *consts)


def _gather_body(rows_per_step, idx_ref, src_ref, dst_ref, sem):
    base = pl.program_id(0) * rows_per_step

    def issue(r, carry):
        pltpu.make_async_copy(src_ref.at[idx_ref[0, 0, r]], dst_ref.at[base + r], sem).start()
        return carry

    lax.fori_loop(0, rows_per_step, issue, 0)
    pltpu.make_async_copy(src_ref.at[pl.ds(0, rows_per_step)], dst_ref.at[pl.ds(base, rows_per_step)], sem).wait()


def _gather_rows(src, idx):
    n = idx.shape[0]
    g = min(GATHER_ROWS, n)
    steps = n // g
    src = src.reshape(-1, ROW_SUB, LANES)
    out = pl.pallas_call(
        functools.partial(_gather_body, g),
        grid=(steps,),
        in_specs=[pl.BlockSpec((1, 1, g), lambda i: (i, 0, 0), memory_space=pltpu.SMEM),
                  pl.BlockSpec(memory_space=pl.ANY)],
        out_specs=pl.BlockSpec(memory_space=pl.ANY),
        out_shape=jax.ShapeDtypeStruct((n, ROW_SUB, LANES), src.dtype),
        scratch_shapes=[pltpu.SemaphoreType.DMA(())],
        compiler_params=pltpu.CompilerParams(dimension_semantics=("arbitrary",)),
        name="row_gather")(idx.reshape(steps, 1, g), src)
    return out.reshape(n * ROW_SUB, LANES)


def _moe_body(blk_e_ref, n_used_ref, xs_ref, wg_ref, wu_ref, wd_ref, o_ref):
    i = pl.program_id(0)

    @pl.when(i < n_used_ref[0])
    def _():
        xb = _load_rows(xs_ref, xs_ref.shape[0] // ROW_SUB).astype(BF16)
        g = _dot(xb, wg_ref[...])
        u = _dot(xb, wu_ref[...])
        _store_rows(o_ref, _dot((_silu(g) * u).astype(BF16), wd_ref[...]))

    @pl.when(i >= n_used_ref[0])
    def _():
        o_ref[...] = jnp.zeros(o_ref.shape, F32)


def _moe_experts(xs, blk_e, n_used, wg_bf, wu_bf, wd_bf, layer, bm):
    n_rows = xs.shape[0] // ROW_SUB
    return pl.pallas_call(
        _moe_body,
        grid_spec=pltpu.PrefetchScalarGridSpec(
            num_scalar_prefetch=2, grid=(n_rows // bm,),
            in_specs=[pl.BlockSpec((bm * ROW_SUB, LANES), lambda i, be, nu: (i, 0)),
                      pl.BlockSpec((None, None, D_MODEL, EXPERT_FF), lambda i, be, nu: (layer, be[i], 0, 0)),
                      pl.BlockSpec((None, None, D_MODEL, EXPERT_FF), lambda i, be, nu: (layer, be[i], 0, 0)),
                      pl.BlockSpec((None, None, EXPERT_FF, D_MODEL), lambda i, be, nu: (layer, be[i], 0, 0))],
            out_specs=pl.BlockSpec((bm * ROW_SUB, LANES), lambda i, be, nu: (i, 0))),
        out_shape=jax.ShapeDtypeStruct((n_rows * ROW_SUB, LANES), F32),
        compiler_params=_params(("arbitrary",)), name="moe_experts")(blk_e, n_used, xs, wg_bf, wu_bf, wd_bf)


def _dispatch_tables(eid, T, bm):
    n_assign = T * TOP_K
    flat_e = eid.reshape(n_assign)
    onehot = (flat_e[:, None] == jnp.arange(N_EXPERTS, dtype=jnp.int32)[None, :]).astype(jnp.int32)
    csum = jnp.cumsum(onehot, axis=0)
    rank = jnp.sum(csum * onehot, axis=1) - 1
    counts = csum[-1]
    padded = ((counts + bm - 1) // bm) * bm
    pends = jnp.cumsum(padded)
    pstarts = pends - padded
    dest = (jnp.sum(onehot * pstarts[None, :], axis=1) + rank).astype(jnp.int32)
    n_blocks = n_assign // bm + N_EXPERTS
    tok = jnp.arange(n_assign, dtype=jnp.int32) // TOP_K
    row_tok = jnp.zeros((n_blocks * bm,), jnp.int32).at[dest].set(tok)
    blk_start = jnp.arange(n_blocks, dtype=jnp.int32) * bm
    blk_e = jnp.minimum(jnp.searchsorted(pends, blk_start, side='right'), N_EXPERTS - 1).astype(jnp.int32)
    n_used = (pends[-1] // bm).astype(jnp.int32).reshape(1)
    return row_tok, blk_e, n_used, dest


def _moe(h2, eid, wg_bf, wu_bf, wd_bf, layer):
    T = h2.shape[0] // ROW_SUB
    bm = min(MOE_BM, T)
    row_tok, blk_e, n_used, dest = _dispatch_tables(eid, T, bm)
    xs = _gather_rows(h2, row_tok)
    ys = _moe_experts(xs, blk_e, n_used, wg_bf, wu_bf, wd_bf, layer, bm)
    return _gather_rows(ys, dest.reshape(T, TOP_K).T.reshape(-1))


def _final_body(x_ref, y0_ref, y1_ref, gate_ref, g_ref, o_ref):
    gate = gate_ref[...]
    n = x_ref.shape[0]
    x = x_ref[...] + gate[:, 0:1] * _load_rows(y0_ref, n) + gate[:, 1:2] * _load_rows(y1_ref, n)
    o_ref[...] = x * lax.rsqrt(jnp.mean(x * x, axis=-1, keepdims=True) + NORM_EPS) * g_ref[...]


def _final_norm(x, ys, gate, g):
    T = x.shape[0]
    tm = min(ROW_TILE, T)
    nt = T // tm
    row = lambda i: (i, 0)
    return pl.pallas_call(
        _final_body,
        grid=(nt,),
        in_specs=[pl.BlockSpec((tm, D_MODEL), row), pl.BlockSpec((tm * ROW_SUB, LANES), row),
                  pl.BlockSpec((tm * ROW_SUB, LANES), lambda i: (i + nt, 0)), pl.BlockSpec((tm, LANES), row),
                  pl.BlockSpec((1, D_MODEL), lambda i: (0, 0))],
        out_specs=pl.BlockSpec((tm, D_MODEL), row),
        out_shape=jax.ShapeDtypeStruct((T, D_MODEL), F32),
        compiler_params=_params(("parallel",)), name="combine_final_norm")(x, ys, ys, gate, g.reshape(1, D_MODEL))


def kernel(x, norm1_g, w_in, hgrn_lower_bounds, hgrn_norm_g, attn_q_norm_g, attn_k_norm_g, s5_a_re, s5_a_im, s5_log_dt, s5_b_re, s5_b_im, s5_c_re, s5_c_im, s5_d, s5_w_glu, s5_b_glu, w_out, norm2_g, router_group_w, router_group_b, router_expert_w, router_expert_b, expert_w_gate, expert_w_up, expert_w_down, final_norm_g):
    B, S, D = x.shape
    T = B * S
    depth = w_in.shape[0]
    cs = _rope_tables(S)
    lb_all = jnp.cumsum(jax.nn.softmax(hgrn_lower_bounds.astype(F32), axis=0), axis=0)
    lb_all = lb_all - lb_all[0:1]
    wg_bf, wu_bf, wd_bf = expert_w_gate.astype(BF16), expert_w_up.astype(BF16), expert_w_down.astype(BF16)
    reps = LANES // HEAD_DIM
    pad = ROUTER_LANES - N_GROUPS - N_EXPERTS

    xt = x.reshape(T, D)
    moe = None
    for l in range(depth):
        if moe is None:
            hg, qkv, u = _in_proj(xt, norm1_g[l], w_in[l].astype(BF16))
        else:
            xt, hg, qkv, u = _in_proj(xt, norm1_g[l], w_in[l].astype(BF16), moe)
        lb = lb_all[l]
        lb_rows = jnp.stack([jnp.log(lb[0]), jnp.log1p(-lb[0]), jnp.log(lb[1]), jnp.log1p(-lb[1])])
        oa = _hgrn(hg, lb_rows, jnp.tile(hgrn_norm_g[l].astype(F32), HG_HEADS).reshape(1, HG_WIDTH), B, S)
        gains = jnp.stack([jnp.tile(attn_q_norm_g[l].astype(F32), reps), jnp.tile(attn_k_norm_g[l].astype(F32), reps)])
        ob = _attention(qkv, cs, gains, B, S)
        bmat, lam_b, cmat = _s5_operators(s5_a_re[l], s5_a_im[l], s5_log_dt[l], s5_b_re[l], s5_b_im[l],
                                          s5_c_re[l], s5_c_im[l], B)
        yf, yb = _s5_scan(u.reshape(B, S, S5_WIDTH), bmat, lam_b, cmat, B, S)
        wr = jnp.concatenate([router_group_w[l].astype(F32), router_expert_w[l].astype(F32),
                              jnp.zeros((D, pad), F32)], axis=1)
        br = jnp.concatenate([router_group_b[l].astype(F32), router_expert_b[l].astype(F32),
                              jnp.zeros((pad,), F32)]).reshape(1, ROUTER_LANES)
        xt, h2, eid, gate = _out_proj(
            xt, oa, ob, u, yf.reshape(T, S5_WIDTH), yb.reshape(T, S5_WIDTH),
            s5_d[l].astype(F32).reshape(1, S5_WIDTH), s5_w_glu[l].astype(BF16),
            s5_b_glu[l].astype(F32).reshape(1, S5_WIDTH), w_out[l].astype(BF16),
            norm2_g[l].astype(F32).reshape(1, D), wr, br)
        moe = (_moe(h2, eid[:, :TOP_K], wg_bf, wu_bf, wd_bf, l), gate)
    out = _final_norm(xt, moe[0], moe[1], final_norm_g.astype(F32))
    return out.reshape(B, S, D)
```

```python
import functools
import math

import jax
import jax.numpy as jnp
from jax import lax
from jax.experimental import pallas as pl
from jax.experimental.pallas import tpu as pltpu

F32 = jnp.float32
BF16 = jnp.bfloat16
HIGHEST = lax.Precision.HIGHEST

D_MODEL = 1024
HEAD_DIM = 64
HG_WIDTH = 256
S5_WIDTH = 256
ATTN_WIDTH = 512
KV_WIDTH = 128
HG_HEADS = HG_WIDTH // HEAD_DIM
HG_CHUNK = 16
ROPE_THETA = 10000.0
GRID_W = 64
S5_CH = 16
S5_GROUPS = S5_WIDTH // S5_CH
S5_STATE = 64
N_GROUPS = 4
PER_GROUP = 8
N_EXPERTS = N_GROUPS * PER_GROUP
TOP_K = 2
EXPERT_FF = D_MODEL // 2
NORM_EPS = 1e-6
HG_COLS = 5 * HG_WIDTH
QKV_COLS = ATTN_WIDTH + 2 * KV_WIDTH
IN_WIDTH = HG_COLS + QKV_COLS + S5_WIDTH
S5_REAL = 2 * S5_GROUPS * S5_STATE

LANES = 128
HG_BLOCK = 128
ATTN_QB = 128
S5_LT = 32
ROW_TILE = 512
MOE_BM = 256
ROUTER_LANES = 128
VMEM_LIMIT = 56 * 1024 * 1024


def _dot(a, b):
    return jnp.dot(a, b, preferred_element_type=F32)


def _dot_hi(a, b):
    return jnp.dot(a, b, preferred_element_type=F32, precision=HIGHEST)


def _dot_nt(a, b):
    return lax.dot_general(a, b, (((1,), (1,)), ((), ())), preferred_element_type=F32)


def _silu(x):
    return x * (1.0 / (1.0 + jnp.exp(-x)))


def _params(sem):
    return pltpu.CompilerParams(dimension_semantics=sem, vmem_limit_bytes=VMEM_LIMIT)


ROW_SUB = D_MODEL // LANES


def _load_rows(ref, n):
    return jnp.concatenate([ref[pl.ds(j, n, stride=ROW_SUB), :] for j in range(ROW_SUB)], axis=1)


def _store_rows(ref, val):
    n = val.shape[0]
    for j in range(ROW_SUB):
        ref[pl.ds(j, n, stride=ROW_SUB), :] = val[:, j * LANES:(j + 1) * LANES]


def _in_proj_body(combine, *refs):
    if combine:
        x_ref, y0_ref, y1_ref, gate_ref, g_ref, w_ref, xo_ref, hg_ref, qkv_ref, u_ref = refs
        gate = gate_ref[...]
        n = x_ref.shape[0]
        x = x_ref[...] + gate[:, 0:1] * _load_rows(y0_ref, n) + gate[:, 1:2] * _load_rows(y1_ref, n)
        xo_ref[...] = x
    else:
        x_ref, g_ref, w_ref, hg_ref, qkv_ref, u_ref = refs
        x = x_ref[...]
    h = x * lax.rsqrt(jnp.mean(x * x, axis=-1, keepdims=True) + NORM_EPS) * g_ref[...]
    hb = h.astype(BF16)
    hg_ref[...] = _dot(hb, w_ref[:, :HG_COLS])
    qkv_ref[...] = _dot(hb, w_ref[:, HG_COLS:HG_COLS + QKV_COLS])
    u_ref[...] = _dot(hb, w_ref[:, HG_COLS + QKV_COLS:])


def _in_proj(x, norm_g, w_bf, moe=None):
    T = x.shape[0]
    tm = min(ROW_TILE, T)
    nt = T // tm
    row = lambda i: (i, 0)
    fixed = lambda i: (0, 0)
    in_specs = [pl.BlockSpec((tm, D_MODEL), row)]
    args = [x]
    out_shape = []
    out_specs = []
    if moe is not None:
        ys, gate = moe
        in_specs += [pl.BlockSpec((tm * ROW_SUB, LANES), row),
                     pl.BlockSpec((tm * ROW_SUB, LANES), lambda i: (i + nt, 0)),
                     pl.BlockSpec((tm, LANES), row)]
        args += [ys, ys, gate]
        out_shape.append(jax.ShapeDtypeStruct((T, D_MODEL), F32))
        out_specs.append(pl.BlockSpec((tm, D_MODEL), row))
    in_specs += [pl.BlockSpec((1, D_MODEL), fixed), pl.BlockSpec((D_MODEL, IN_WIDTH), fixed)]
    args += [norm_g.reshape(1, D_MODEL), w_bf]
    out_shape += [jax.ShapeDtypeStruct((T, HG_COLS), F32), jax.ShapeDtypeStruct((T, QKV_COLS), F32),
                  jax.ShapeDtypeStruct((T, S5_WIDTH), F32)]
    out_specs += [pl.BlockSpec((tm, HG_COLS), row), pl.BlockSpec((tm, QKV_COLS), row),
                  pl.BlockSpec((tm, S5_WIDTH), row)]
    return pl.pallas_call(
        functools.partial(_in_proj_body, moe is not None),
        grid=(nt,), in_specs=in_specs, out_specs=out_specs, out_shape=out_shape,
        compiler_params=_params(("parallel",)), name="in_proj")(*args)


def _log_forget_and_key(z, log_lb, log_1m_lb):
    log_sig = jnp.minimum(z, 0.0) - jnp.log1p(jnp.exp(-jnp.abs(z)))
    a = log_1m_lb + log_sig
    log_f = jnp.maximum(a, log_lb) + jnp.log1p(jnp.exp(-jnp.abs(a - log_lb)))
    return log_f, jnp.exp(a - z)


def _split3(x):
    hi = x.astype(BF16)
    r = x - hi.astype(F32)
    mid = r.astype(BF16)
    lo = (r - mid.astype(F32)).astype(BF16)
    return hi, mid, lo


def _hgrn_body(hg_ref, lb_ref, gn_ref, o_ref, of_s, ob_s, st_s, kv_s):
    S = hg_ref.shape[0]
    n_blk = S // HG_BLOCK
    n_sub = HG_BLOCK // HG_CHUNK
    row = lax.broadcasted_iota(jnp.int32, (HG_BLOCK, HG_BLOCK), 0)
    col = lax.broadcasted_iota(jnp.int32, (HG_BLOCK, HG_BLOCK), 1)
    same = (row // HG_CHUNK) == (col // HG_CHUNK)
    lane_head = lax.broadcasted_iota(jnp.int32, (1, HG_WIDTH), 1) // HEAD_DIM
    col_sub = lax.broadcasted_iota(jnp.int32, (HG_WIDTH, HG_BLOCK), 1) // HG_CHUNK
    same_bf = jnp.where(same, 1.0, 0.0).astype(BF16)

    st_s[...] = jnp.zeros(st_s.shape, F32)

    def block(d, j, out_s):
        r0 = pl.multiple_of(j * HG_BLOCK, HG_BLOCK)
        rows = pl.ds(r0, HG_BLOCK)
        tri = same & ((col <= row) if d == 0 else (col >= row))
        z = hg_ref[rows, (3 + d) * HG_WIDTH:(4 + d) * HG_WIDTH]
        lf, k = _log_forget_and_key(z, lb_ref[2 * d:2 * d + 1, :], lb_ref[2 * d + 1:2 * d + 2, :])
        a_mat = jnp.concatenate([jnp.where(tri, 1.0, 0.0).astype(BF16), same_bf], axis=0)
        hi, mid, lo = _split3(lf)
        acc = _dot(a_mat, hi) + _dot(a_mat, mid) + _dot(a_mat, lo)
        b = acc[:HG_BLOCK]
        tot = acc[HG_BLOCK:]
        q = _silu(hg_ref[rows, 0:HG_WIDTH])
        v = hg_ref[rows, HG_WIDTH:2 * HG_WIDTH]
        q_dec = q * jnp.exp(b)
        k_dec = (k * jnp.exp(-b)).astype(BF16)
        k_end = (k * jnp.exp(tot - b)).astype(BF16)
        dec = jnp.exp(tot)
        q_heads = jnp.concatenate(
            [jnp.where(lane_head == h, q_dec, 0.0).astype(BF16) for h in range(HG_HEADS)], axis=0)
        scores = _dot_nt(q_heads, k_dec)
        o_blk = jnp.zeros((HG_BLOCK, HG_WIDTH), F32)
        for h in range(HG_HEADS):
            p_h = jnp.where(tri, scores[h * HG_BLOCK:(h + 1) * HG_BLOCK], 0.0).astype(BF16)
            o_blk += _dot(p_h, jnp.where(lane_head == h, v, 0.0).astype(BF16))
        v_t = v.T
        v_sub = jnp.concatenate(
            [jnp.where(col_sub == c, v_t, 0.0).astype(BF16) for c in range(n_sub)], axis=0)
        kv_s[...] = _dot(v_sub, k_end)
        o_inter = [None] * n_sub
        order = range(n_sub) if d == 0 else range(n_sub - 1, -1, -1)
        for c in order:
            q_c = jnp.concatenate(
                [q_heads[h * HG_BLOCK + c * HG_CHUNK:h * HG_BLOCK + (c + 1) * HG_CHUNK]
                 for h in range(HG_HEADS)], axis=0)
            st = st_s[d]
            o_all = _dot_nt(q_c, st.astype(BF16))
            o_c = jnp.zeros((HG_CHUNK, HG_WIDTH), F32)
            for h in range(HG_HEADS):
                o_c += jnp.where(lane_head == h, o_all[h * HG_CHUNK:(h + 1) * HG_CHUNK], 0.0)
            o_inter[c] = o_c
            st_s[d] = dec[c * HG_CHUNK:c * HG_CHUNK + 1, :] * st + kv_s[c * HG_WIDTH:(c + 1) * HG_WIDTH, :]
        out_s[rows, :] = o_blk + jnp.concatenate(o_inter, axis=0)

    def step(j, carry):
        block(0, j, of_s)
        block(1, n_blk - 1 - j, ob_s)
        return carry

    lax.fori_loop(0, n_blk, step, 0)

    r2 = lax.broadcasted_iota(jnp.int32, (HG_WIDTH, HG_WIDTH), 0) // HEAD_DIM
    c2 = lax.broadcasted_iota(jnp.int32, (HG_WIDTH, HG_WIDTH), 1) // HEAD_DIM
    head_mean = jnp.where(r2 == c2, 1.0 / HEAD_DIM, 0.0).astype(F32)

    def finish(j, carry):
        rows = pl.ds(pl.multiple_of(j * HG_BLOCK, HG_BLOCK), HG_BLOCK)
        o = of_s[rows, :] + ob_s[rows, :]
        ms = _dot_hi(o * o, head_mean)
        y = o * lax.rsqrt(ms + NORM_EPS) * gn_ref[...]
        o_ref[rows, :] = y * _silu(hg_ref[rows, 2 * HG_WIDTH:3 * HG_WIDTH])
        return carry

    lax.fori_loop(0, n_blk, finish, 0)


def _hgrn(hg, lb_rows, gn, B, S):
    return pl.pallas_call(
        _hgrn_body,
        grid=(B,),
        in_specs=[pl.BlockSpec((S, HG_COLS), lambda b: (b, 0)),
                  pl.BlockSpec((4, HG_WIDTH), lambda b: (0, 0)),
                  pl.BlockSpec((1, HG_WIDTH), lambda b: (0, 0))],
        out_specs=pl.BlockSpec((S, HG_WIDTH), lambda b: (b, 0)),
        out_shape=jax.ShapeDtypeStruct((B * S, HG_WIDTH), F32),
        scratch_shapes=[pltpu.VMEM((S, HG_WIDTH), F32), pltpu.VMEM((S, HG_WIDTH), F32),
                        pltpu.VMEM((2, HG_WIDTH, HG_WIDTH), F32),
                        pltpu.VMEM((HG_BLOCK // HG_CHUNK * HG_WIDTH, HG_WIDTH), F32)],
        compiler_params=_params(("parallel",)), name="hgrn2")(hg, lb_rows, gn)


def _attn_body(qkv_ref, cs_ref, gain_ref, o_ref, q_s, k_s, vlo_s, vhi_s):
    S = qkv_ref.shape[0]
    n_qb = S // ATTN_QB
    lane = lax.broadcasted_iota(jnp.int32, (1, LANES), 1)
    lo = lane < HEAD_DIM
    even = (lane % 2) == 0
    r2 = lax.broadcasted_iota(jnp.int32, (LANES, LANES), 0) // HEAD_DIM
    c2 = lax.broadcasted_iota(jnp.int32, (LANES, LANES), 1) // HEAD_DIM
    head_mean = jnp.where(r2 == c2, 1.0 / HEAD_DIM, 0.0).astype(F32)
    n_pairs = ATTN_WIDTH // LANES

    def norm_rope(x, gain, cos, sin):
        ms = _dot_hi(x * x, head_mean)
        y = x * lax.rsqrt(ms + NORM_EPS) * gain
        swapped = jnp.where(even, pltpu.roll(y, LANES - 1, 1), pltpu.roll(y, 1, 1))
        return y * cos + swapped * sin

    def prep(j, carry):
        rows = pl.ds(pl.multiple_of(j * ATTN_QB, ATTN_QB), ATTN_QB)
        cos = cs_ref[0, rows, :]
        sin = cs_ref[1, rows, :]
        for m in range(n_pairs):
            qm = norm_rope(qkv_ref[rows, m * LANES:(m + 1) * LANES], gain_ref[0:1, :], cos, sin)
            qm = qm * (HEAD_DIM ** -0.5)
            q_s[m, 0, rows, :] = jnp.where(lo, qm, 0.0).astype(BF16)
            q_s[m, 1, rows, :] = jnp.where(lo, 0.0, qm).astype(BF16)
        kk = norm_rope(qkv_ref[rows, ATTN_WIDTH:ATTN_WIDTH + LANES], gain_ref[1:2, :], cos, sin)
        kr = pltpu.roll(kk, HEAD_DIM, 1)
        k_s[0, rows, :] = jnp.where(lo, kk, kr).astype(BF16)
        k_s[1, rows, :] = jnp.where(lo, kr, kk).astype(BF16)
        vv = qkv_ref[rows, ATTN_WIDTH + LANES:ATTN_WIDTH + 2 * LANES]
        vr = pltpu.roll(vv, HEAD_DIM, 1)
        vlo_s[0, rows, :] = jnp.where(lo, vv, 0.0).astype(BF16)
        vhi_s[0, rows, :] = jnp.where(lo, 0.0, vr).astype(BF16)
        vlo_s[1, rows, :] = jnp.where(lo, vr, 0.0).astype(BF16)
        vhi_s[1, rows, :] = jnp.where(lo, 0.0, vv).astype(BF16)
        return carry

    lax.fori_loop(0, n_qb, prep, 0)

    def attend(j, carry):
        rows = pl.ds(pl.multiple_of(j * ATTN_QB, ATTN_QB), ATTN_QB)
        for m in range(n_pairs):
            kv = m // (n_pairs // 2)
            q2 = jnp.concatenate([q_s[m, 0, rows, :], q_s[m, 1, rows, :]], axis=0)
            s = _dot_nt(q2, k_s[kv])
            p = jnp.exp(s - jnp.max(s, axis=-1, keepdims=True))
            l = jnp.sum(p, axis=-1, keepdims=True)
            pb = p.astype(BF16)
            o = _dot(pb[:ATTN_QB], vlo_s[kv]) + _dot(pb[ATTN_QB:], vhi_s[kv])
            inv = jnp.where(lo, 1.0 / l[:ATTN_QB], 1.0 / l[ATTN_QB:])
            o_ref[rows, m * LANES:(m + 1) * LANES] = o * inv
        return carry

    lax.fori_loop(0, n_qb, attend, 0)


def _attention(qkv, cs, gains, B, S):
    n_pairs = ATTN_WIDTH // LANES
    return pl.pallas_call(
        _attn_body,
        grid=(B,),
        in_specs=[pl.BlockSpec((S, QKV_COLS), lambda b: (b, 0)),
                  pl.BlockSpec((2, S, LANES), lambda b: (0, 0, 0)),
                  pl.BlockSpec((2, LANES), lambda b: (0, 0))],
        out_specs=pl.BlockSpec((S, ATTN_WIDTH), lambda b: (b, 0)),
        out_shape=jax.ShapeDtypeStruct((B * S, ATTN_WIDTH), F32),
        scratch_shapes=[pltpu.VMEM((n_pairs, 2, S, LANES), BF16), pltpu.VMEM((2, S, LANES), BF16),
                        pltpu.VMEM((2, S, LANES), BF16), pltpu.VMEM((2, S, LANES), BF16)],
        compiler_params=_params(("parallel",)), name="gqa_attention")(qkv, cs, gains)


def _rope_tables(S):
    rows_count = S // GRID_W
    rows = jnp.repeat(jnp.arange(rows_count, dtype=F32), GRID_W)
    cols = jnp.tile(jnp.arange(GRID_W, dtype=F32), rows_count)
    pairs = HEAD_DIM // 4
    freqs = jnp.power(jnp.float32(ROPE_THETA), -jnp.arange(pairs, dtype=F32) / pairs)
    ang = jnp.concatenate([rows[:, None] * freqs, cols[:, None] * freqs], axis=-1)
    cos = jnp.repeat(jnp.cos(ang), 2, axis=-1)
    sin = jnp.repeat(jnp.sin(ang), 2, axis=-1) * jnp.tile(jnp.array([-1.0, 1.0], F32), HEAD_DIM // 2)
    reps = LANES // HEAD_DIM
    return jnp.stack([jnp.tile(cos, (1, reps)), jnp.tile(sin, (1, reps))])


def _s5_body(uf_ref, ub_ref, bmat_ref, lam_ref, cmat_ref, yf_ref, yb_ref, st_s, x_s):
    B, lt, _ = uf_ref.shape
    half = S5_REAL // 2

    @pl.when(pl.program_id(0) == 0)
    def _():
        st_s[...] = jnp.zeros(st_s.shape, F32)

    n_tiles = half // LANES

    def direction(d, u_ref, y_ref):
        u = u_ref[...].reshape(B * lt, S5_WIDTH).astype(BF16)
        bu = _dot(u, bmat_ref[d])
        for c in range(2 * n_tiles):
            x_s[c] = bu[:, c * LANES:(c + 1) * LANES]
        order = range(lt) if d == 0 else range(lt - 1, -1, -1)
        re_t = lambda c: slice(c * LANES, (c + 1) * LANES)
        im_t = lambda c: slice(half + c * LANES, half + (c + 1) * LANES)
        xr = [st_s[d, :, re_t(c)] for c in range(n_tiles)]
        xi = [st_s[d, :, im_t(c)] for c in range(n_tiles)]
        for t in order:
            slab = pl.ds(t, B, stride=lt)
            for c in range(n_tiles):
                lr = lam_ref[d, :, re_t(c)]
                li = lam_ref[d, :, im_t(c)]
                nr = lr * xr[c] - li * xi[c] + x_s[c, slab, :]
                ni = lr * xi[c] + li * xr[c] + x_s[n_tiles + c, slab, :]
                xr[c], xi[c] = nr, ni
                x_s[c, slab, :] = nr
                x_s[n_tiles + c, slab, :] = ni
        for c in range(n_tiles):
            st_s[d, :, re_t(c)] = xr[c]
            st_s[d, :, im_t(c)] = xi[c]
        xs = jnp.concatenate([x_s[c].astype(BF16) for c in range(2 * n_tiles)], axis=1)
        y_ref[...] = _dot(xs, cmat_ref[d]).reshape(B, lt, S5_WIDTH)

    direction(0, uf_ref, yf_ref)
    direction(1, ub_ref, yb_ref)


def _s5_scan(u, bmat, lam_b, cmat, B, S):
    lt = min(S5_LT, S)
    n = S // lt
    fwd = lambda j: (0, j, 0)
    bwd = lambda j: (0, n - 1 - j, 0)
    fixed = lambda j: (0, 0, 0)
    return pl.pallas_call(
        _s5_body,
        grid=(n,),
        in_specs=[pl.BlockSpec((B, lt, S5_WIDTH), fwd), pl.BlockSpec((B, lt, S5_WIDTH), bwd),
                  pl.BlockSpec((2, S5_WIDTH, S5_REAL), fixed), pl.BlockSpec((2, B, S5_REAL), fixed),
                  pl.BlockSpec((2, S5_REAL, S5_WIDTH), fixed)],
        out_specs=[pl.BlockSpec((B, lt, S5_WIDTH), fwd), pl.BlockSpec((B, lt, S5_WIDTH), bwd)],
        out_shape=[jax.ShapeDtypeStruct((B, S, S5_WIDTH), F32)] * 2,
        scratch_shapes=[pltpu.VMEM((2, B, S5_REAL), F32), pltpu.VMEM((S5_REAL // LANES, B * lt, LANES), F32)],
        compiler_params=_params(("arbitrary",)), name="s5_scan")(u, u, bmat, lam_b, cmat)


def _s5_operators(a_re, a_im, log_dt, b_re, b_im, c_re, c_im, B):
    dt = jnp.exp(log_dt.astype(F32))[..., None]
    ar, ai = a_re.astype(F32), a_im.astype(F32)
    mag = jnp.exp(ar * dt)
    lr, li = mag * jnp.cos(ai * dt), mag * jnp.sin(ai * dt)
    den = ar * ar + ai * ai
    fr = ((lr - 1.0) * ar + li * ai) / den
    fi = (li * ar - (lr - 1.0) * ai) / den
    br, bi = b_re.astype(F32), b_im.astype(F32)
    bbr = fr[..., None] * br - fi[..., None] * bi
    bbi = fr[..., None] * bi + fi[..., None] * br
    eye = jnp.eye(S5_GROUPS, dtype=F32)
    in_blk = lambda m: jnp.einsum('zgpc,gh->zgchp', m, eye).reshape(2, S5_WIDTH, S5_REAL // 2)
    out_blk = lambda m: jnp.einsum('zgcp,gh->zgphc', m, eye).reshape(2, S5_REAL // 2, S5_WIDTH)
    bmat = jnp.concatenate([in_blk(bbr), in_blk(bbi)], axis=-1).astype(BF16)
    cmat = jnp.concatenate([out_blk(c_re.astype(F32)), -out_blk(c_im.astype(F32))], axis=1).astype(BF16)
    lam = jnp.concatenate([lr.reshape(2, 1, -1), li.reshape(2, 1, -1)], axis=-1)
    return bmat, jnp.broadcast_to(lam, (2, B, S5_REAL)), cmat


def _gelu_tanh(x):
    return 0.5 * x * (1.0 + jnp.tanh(math.sqrt(2.0 / math.pi) * (x + 0.044715 * (x * x * x))))


def _out_proj_body(x_ref, oa_ref, ob_ref, u_ref, yf_ref, yb_ref, dskip_ref, wglu_ref, bglu_ref, wout_ref,
                   g2_ref, wr_ref, br_ref, x1_ref, h2_ref, eid_ref, gate_ref):
    y = dskip_ref[...] * u_ref[...] + yf_ref[...] + yb_ref[...]
    z = _gelu_tanh(y)
    t = _dot(z.astype(BF16), wglu_ref[...]) + bglu_ref[...]
    oc = z * (1.0 / (1.0 + jnp.exp(-t)))
    mix = _dot(oa_ref[...].astype(BF16), wout_ref[0:HG_WIDTH, :])
    mix += _dot(ob_ref[...].astype(BF16), wout_ref[HG_WIDTH:HG_WIDTH + ATTN_WIDTH, :])
    mix += _dot(oc.astype(BF16), wout_ref[HG_WIDTH + ATTN_WIDTH:, :])
    x1 = x_ref[...] + mix
    x1_ref[...] = x1
    h2 = x1 * lax.rsqrt(jnp.mean(x1 * x1, axis=-1, keepdims=True) + NORM_EPS) * g2_ref[...]
    _store_rows(h2_ref, h2)
    logits = _dot_hi(h2, wr_ref[...]) + br_ref[...]
    lane = lax.broadcasted_iota(jnp.int32, logits.shape, 1)
    lane_f = lane.astype(F32)
    neg = jnp.float32(-jnp.inf)
    is_g = lane < N_GROUPS
    gl = jnp.where(is_g, logits, neg)
    gmax = jnp.max(gl, axis=-1, keepdims=True)
    gsel = jnp.min(jnp.where(gl == gmax, lane_f, float(ROUTER_LANES)), axis=-1, keepdims=True)
    p_group = 1.0 / jnp.sum(jnp.where(is_g, jnp.exp(logits - gmax), 0.0), axis=-1, keepdims=True)
    first = N_GROUPS + PER_GROUP * gsel
    el = jnp.where((lane_f >= first) & (lane_f < first + PER_GROUP), logits, neg)
    v1 = jnp.max(el, axis=-1, keepdims=True)
    i1 = jnp.min(jnp.where(el == v1, lane_f, float(ROUTER_LANES)), axis=-1, keepdims=True)
    el2 = jnp.where(lane_f == i1, neg, el)
    v2 = jnp.max(el2, axis=-1, keepdims=True)
    i2 = jnp.min(jnp.where(el2 == v2, lane_f, float(ROUTER_LANES)), axis=-1, keepdims=True)
    e2 = jnp.exp(v2 - v1)
    w1 = p_group / (1.0 + e2)
    w2 = p_group * e2 / (1.0 + e2)
    eid_ref[...] = jnp.where(lane == 0, i1 - N_GROUPS, jnp.where(lane == 1, i2 - N_GROUPS, 0.0)).astype(jnp.int32)
    gate_ref[...] = jnp.where(lane == 0, w1, jnp.where(lane == 1, w2, 0.0))


def _out_proj(x, oa, ob, u, yf, yb, dskip, wglu_bf, bglu, wout_bf, g2, wr, br):
    T = x.shape[0]
    tm = min(ROW_TILE, T)
    row = lambda i: (i, 0)
    fixed = lambda i: (0, 0)
    blk = lambda w: pl.BlockSpec((tm, w), row)
    full = lambda a: pl.BlockSpec(a.shape, fixed)
    consts = [dskip, wglu_bf, bglu, wout_bf, g2, wr, br]
    return pl.pallas_call(
        _out_proj_body,
        grid=(T // tm,),
        in_specs=[blk(D_MODEL), blk(HG_WIDTH), blk(ATTN_WIDTH), blk(S5_WIDTH), blk(S5_WIDTH), blk(S5_WIDTH)]
        + [full(a) for a in consts],
        out_specs=[blk(D_MODEL), pl.BlockSpec((tm * ROW_SUB, LANES), row), blk(ROUTER_LANES), blk(ROUTER_LANES)],
        out_shape=[jax.ShapeDtypeStruct((T, D_MODEL), F32), jax.ShapeDtypeStruct((T * ROW_SUB, LANES), F32),
                   jax.ShapeDtypeStruct((T, ROUTER_LANES), jnp.int32),
                   jax.ShapeDtypeStruct((T, ROUTER_LANES), F32)],
        compiler_params=_params(("parallel",)), name="out_proj_router")(x, oa, ob, u, yf, yb, *consts)


def _moe_body(bm, blk_e_ref, n_used_ref, cnt_ref, tok_ref, tok_next_ref, dst_ref, h2_hbm, wg_ref, wu_ref, wd_ref,
              out_hbm, xbuf, obuf, gsem, ssem):
    i = pl.program_id(0)
    n_used = n_used_ref[0]
    slot = i % 2

    def gather(idx_ref, s):
        def issue(r, carry):
            pltpu.make_async_copy(h2_hbm.at[idx_ref[0, 0, r]],
                                  xbuf.at[s, pl.ds(pl.multiple_of(r * ROW_SUB, ROW_SUB), ROW_SUB)],
                                  gsem.at[s]).start()
            return carry
        lax.fori_loop(0, bm, issue, 0, unroll=8)

    def scatter_wait(s, n):
        @pl.when(n > 0)
        def _():
            rows = pl.ds(0, n * ROW_SUB)
            pltpu.make_async_copy(obuf.at[s, rows], obuf.at[s, rows], ssem.at[s]).wait()

    @pl.when(i == 0)
    def _():
        gather(tok_ref, 0)

    @pl.when(i + 1 < n_used)
    def _():
        gather(tok_next_ref, 1 - slot)

    @pl.when(i < n_used)
    def _():
        pltpu.make_async_copy(xbuf.at[slot], xbuf.at[slot], gsem.at[slot]).wait()
        xb = _load_rows(xbuf.at[slot], bm).astype(BF16)
        g = _dot(xb, wg_ref[...])
        u = _dot(xb, wu_ref[...])
        _store_rows(obuf.at[slot], _dot((_silu(g) * u).astype(BF16), wd_ref[...]))
        cnt = cnt_ref[i]

        def issue(r, carry):
            pltpu.make_async_copy(obuf.at[slot, pl.ds(pl.multiple_of(r * ROW_SUB, ROW_SUB), ROW_SUB)],
                                  out_hbm.at[dst_ref[0, 0, r]], ssem.at[slot]).start()
            return carry
        lax.fori_loop(0, cnt, issue, 0)

        @pl.when(i > 0)
        def _():
            scatter_wait(1 - slot, cnt_ref[i - 1])

        @pl.when(i == n_used - 1)
        def _():
            scatter_wait(slot, cnt)


def _moe_experts(h2, tables, wg_bf, wu_bf, wd_bf, layer, bm, T):
    row_tok, row_dst, blk_e, blk_cnt, n_used = tables
    n_blocks = blk_e.shape[0]
    tok3 = row_tok.reshape(n_blocks, 1, bm)
    dst3 = row_dst.reshape(n_blocks, 1, bm)
    smem_blk = lambda f: pl.BlockSpec((1, 1, bm), f, memory_space=pltpu.SMEM)
    wspec = lambda shape: pl.BlockSpec((None, None) + shape, lambda i, be, nu, cn: (layer, be[i], 0, 0))
    out = pl.pallas_call(
        functools.partial(_moe_body, bm),
        grid_spec=pltpu.PrefetchScalarGridSpec(
            num_scalar_prefetch=3, grid=(n_blocks,),
            in_specs=[smem_blk(lambda i, be, nu, cn: (i, 0, 0)),
                      smem_blk(lambda i, be, nu, cn: (jnp.minimum(i + 1, n_blocks - 1), 0, 0)),
                      smem_blk(lambda i, be, nu, cn: (i, 0, 0)),
                      pl.BlockSpec(memory_space=pl.ANY),
                      wspec((D_MODEL, EXPERT_FF)), wspec((D_MODEL, EXPERT_FF)), wspec((EXPERT_FF, D_MODEL))],
            out_specs=pl.BlockSpec(memory_space=pl.ANY),
            scratch_shapes=[pltpu.VMEM((2, bm * ROW_SUB, LANES), F32), pltpu.VMEM((2, bm * ROW_SUB, LANES), F32),
                            pltpu.SemaphoreType.DMA((2,)), pltpu.SemaphoreType.DMA((2,))]),
        out_shape=jax.ShapeDtypeStruct((TOP_K * T, ROW_SUB, LANES), F32),
        compiler_params=_params(("arbitrary",)), name="moe_experts")(
            blk_e, n_used, blk_cnt, tok3, tok3, dst3, h2.reshape(T, ROW_SUB, LANES), wg_bf, wu_bf, wd_bf)
    return out.reshape(TOP_K * T * ROW_SUB, LANES)


def _dispatch_tables(eid, T, bm):
    n_assign = T * TOP_K
    flat_e = eid.reshape(n_assign)
    onehot = (flat_e[:, None] == jnp.arange(N_EXPERTS, dtype=jnp.int32)[None, :]).astype(jnp.int32)
    csum = jnp.cumsum(onehot, axis=0)
    rank = jnp.sum(csum * onehot, axis=1) - 1
    counts = csum[-1]
    padded = ((counts + bm - 1) // bm) * bm
    pends = jnp.cumsum(padded)
    pstarts = pends - padded
    dest = (jnp.sum(onehot * pstarts[None, :], axis=1) + rank).astype(jnp.int32)
    n_blocks = n_assign // bm + N_EXPERTS
    assign = jnp.arange(n_assign, dtype=jnp.int32)
    row_assign = jnp.zeros((n_blocks * bm,), jnp.int32).at[dest].set(assign)
    row_tok = row_assign // TOP_K
    row_dst = (row_assign % TOP_K) * T + row_tok
    blk_start = jnp.arange(n_blocks, dtype=jnp.int32) * bm
    blk_e = jnp.minimum(jnp.searchsorted(pends, blk_start, side='right'), N_EXPERTS - 1).astype(jnp.int32)
    blk_cnt = jnp.clip(counts[blk_e] - (blk_start - pstarts[blk_e]), 0, bm).astype(jnp.int32)
    n_used = (pends[-1] // bm).astype(jnp.int32).reshape(1)
    return row_tok, row_dst, blk_e, blk_cnt, n_used


def _moe(h2, eid, wg_bf, wu_bf, wd_bf, layer):
    T = h2.shape[0] // ROW_SUB
    bm = min(MOE_BM, T)
    return _moe_experts(h2, _dispatch_tables(eid, T, bm), wg_bf, wu_bf, wd_bf, layer, bm, T)


def _final_body(x_ref, y0_ref, y1_ref, gate_ref, g_ref, o_ref):
    gate = gate_ref[...]
    n = x_ref.shape[0]
    x = x_ref[...] + gate[:, 0:1] * _load_rows(y0_ref, n) + gate[:, 1:2] * _load_rows(y1_ref, n)
    o_ref[...] = x * lax.rsqrt(jnp.mean(x * x, axis=-1, keepdims=True) + NORM_EPS) * g_ref[...]


def _final_norm(x, ys, gate, g):
    T = x.shape[0]
    tm = min(ROW_TILE, T)
    nt = T // tm
    row = lambda i: (i, 0)
    return pl.pallas_call(
        _final_body,
        grid=(nt,),
        in_specs=[pl.BlockSpec((tm, D_MODEL), row), pl.BlockSpec((tm * ROW_SUB, LANES), row),
                  pl.BlockSpec((tm * ROW_SUB, LANES), lambda i: (i + nt, 0)), pl.BlockSpec((tm, LANES), row),
                  pl.BlockSpec((1, D_MODEL), lambda i: (0, 0))],
        out_specs=pl.BlockSpec((tm, D_MODEL), row),
        out_shape=jax.ShapeDtypeStruct((T, D_MODEL), F32),
        compiler_params=_params(("parallel",)), name="combine_final_norm")(x, ys, ys, gate, g.reshape(1, D_MODEL))


def kernel(x, norm1_g, w_in, hgrn_lower_bounds, hgrn_norm_g, attn_q_norm_g, attn_k_norm_g, s5_a_re, s5_a_im, s5_log_dt, s5_b_re, s5_b_im, s5_c_re, s5_c_im, s5_d, s5_w_glu, s5_b_glu, w_out, norm2_g, router_group_w, router_group_b, router_expert_w, router_expert_b, expert_w_gate, expert_w_up, expert_w_down, final_norm_g):
    B, S, D = x.shape
    T = B * S
    depth = w_in.shape[0]
    cs = _rope_tables(S)
    lb_all = jnp.cumsum(jax.nn.softmax(hgrn_lower_bounds.astype(F32), axis=0), axis=0)
    lb_all = lb_all - lb_all[0:1]
    wg_bf, wu_bf, wd_bf = expert_w_gate.astype(BF16), expert_w_up.astype(BF16), expert_w_down.astype(BF16)
    reps = LANES // HEAD_DIM
    pad = ROUTER_LANES - N_GROUPS - N_EXPERTS

    xt = x.reshape(T, D)
    moe = None
    for l in range(depth):
        if moe is None:
            hg, qkv, u = _in_proj(xt, norm1_g[l], w_in[l].astype(BF16))
        else:
            xt, hg, qkv, u = _in_proj(xt, norm1_g[l], w_in[l].astype(BF16), moe)
        lb = lb_all[l]
        lb_rows = jnp.stack([jnp.log(lb[0]), jnp.log1p(-lb[0]), jnp.log(lb[1]), jnp.log1p(-lb[1])])
        oa = _hgrn(hg, lb_rows, jnp.tile(hgrn_norm_g[l].astype(F32), HG_HEADS).reshape(1, HG_WIDTH), B, S)
        gains = jnp.stack([jnp.tile(attn_q_norm_g[l].astype(F32), reps), jnp.tile(attn_k_norm_g[l].astype(F32), reps)])
        ob = _attention(qkv, cs, gains, B, S)
        bmat, lam_b, cmat = _s5_operators(s5_a_re[l], s5_a_im[l], s5_log_dt[l], s5_b_re[l], s5_b_im[l],
                                          s5_c_re[l], s5_c_im[l], B)
        yf, yb = _s5_scan(u.reshape(B, S, S5_WIDTH), bmat, lam_b, cmat, B, S)
        wr = jnp.concatenate([router_group_w[l].astype(F32), router_expert_w[l].astype(F32),
                              jnp.zeros((D, pad), F32)], axis=1)
        br = jnp.concatenate([router_group_b[l].astype(F32), router_expert_b[l].astype(F32),
                              jnp.zeros((pad,), F32)]).reshape(1, ROUTER_LANES)
        xt, h2, eid, gate = _out_proj(
            xt, oa, ob, u, yf.reshape(T, S5_WIDTH), yb.reshape(T, S5_WIDTH),
            s5_d[l].astype(F32).reshape(1, S5_WIDTH), s5_w_glu[l].astype(BF16),
            s5_b_glu[l].astype(F32).reshape(1, S5_WIDTH), w_out[l].astype(BF16),
            norm2_g[l].astype(F32).reshape(1, D), wr, br)
        moe = (_moe(h2, eid[:, :TOP_K], wg_bf, wu_bf, wd_bf, l), gate)
    out = _final_norm(xt, moe[0], moe[1], final_norm_g.astype(F32))
    return out.reshape(B, S, D)
```

```python
import functools
import math

import jax
import jax.numpy as jnp
from jax import lax
from jax.experimental import pallas as pl
from jax.experimental.pallas import tpu as pltpu

F32 = jnp.float32
BF16 = jnp.bfloat16
HIGHEST = lax.Precision.HIGHEST

D_MODEL = 1024
HEAD_DIM = 64
HG_WIDTH = 256
S5_WIDTH = 256
ATTN_WIDTH = 512
KV_WIDTH = 128
HG_HEADS = HG_WIDTH // HEAD_DIM
HG_CHUNK = 16
ROPE_THETA = 10000.0
GRID_W = 64
S5_CH = 16
S5_GROUPS = S5_WIDTH // S5_CH
S5_STATE = 64
N_GROUPS = 4
PER_GROUP = 8
N_EXPERTS = N_GROUPS * PER_GROUP
TOP_K = 2
EXPERT_FF = D_MODEL // 2
NORM_EPS = 1e-6
HG_COLS = 5 * HG_WIDTH
QKV_COLS = ATTN_WIDTH + 2 * KV_WIDTH
IN_WIDTH = HG_COLS + QKV_COLS + S5_WIDTH
S5_REAL = 2 * S5_GROUPS * S5_STATE

LANES = 128
HG_BLOCK = 128
ATTN_QB = 128
S5_LT = 32
ROW_TILE = 512
MOE_BM = 256
ROUTER_LANES = 128
VMEM_LIMIT = 56 * 1024 * 1024


def _dot(a, b):
    return jnp.dot(a, b, preferred_element_type=F32)


def _dot_hi(a, b):
    return jnp.dot(a, b, preferred_element_type=F32, precision=HIGHEST)


def _dot_nt(a, b):
    return lax.dot_general(a, b, (((1,), (1,)), ((), ())), preferred_element_type=F32)


def _silu(x):
    return x * (1.0 / (1.0 + jnp.exp(-x)))


def _params(sem):
    return pltpu.CompilerParams(dimension_semantics=sem, vmem_limit_bytes=VMEM_LIMIT)


ROW_SUB = D_MODEL // LANES


def _load_rows(ref, n):
    return jnp.concatenate([ref[pl.ds(j, n, stride=ROW_SUB), :] for j in range(ROW_SUB)], axis=1)


def _store_rows(ref, val):
    n = val.shape[0]
    for j in range(ROW_SUB):
        ref[pl.ds(j, n, stride=ROW_SUB), :] = val[:, j * LANES:(j + 1) * LANES]


def _in_proj_body(combine, *refs):
    if combine:
        x_ref, y0_ref, y1_ref, gate_ref, g_ref, w_ref, xo_ref, hg_ref, qkv_ref, u_ref = refs
        gate = gate_ref[...]
        n = x_ref.shape[0]
        x = x_ref[...] + gate[:, 0:1] * _load_rows(y0_ref, n) + gate[:, 1:2] * _load_rows(y1_ref, n)
        xo_ref[...] = x
    else:
        x_ref, g_ref, w_ref, hg_ref, qkv_ref, u_ref = refs
        x = x_ref[...]
    h = x * lax.rsqrt(jnp.mean(x * x, axis=-1, keepdims=True) + NORM_EPS) * g_ref[...]
    hb = h.astype(BF16)
    hg_ref[...] = _dot(hb, w_ref[:, :HG_COLS])
    qkv_ref[...] = _dot(hb, w_ref[:, HG_COLS:HG_COLS + QKV_COLS])
    u_ref[...] = _dot(hb, w_ref[:, HG_COLS + QKV_COLS:])


def _in_proj(x, norm_g, w_bf, moe=None):
    T = x.shape[0]
    tm = min(ROW_TILE, T)
    nt = T // tm
    row = lambda i: (i, 0)
    fixed = lambda i: (0, 0)
    in_specs = [pl.BlockSpec((tm, D_MODEL), row)]
    args = [x]
    out_shape = []
    out_specs = []
    if moe is not None:
        ys, gate = moe
        in_specs += [pl.BlockSpec((tm * ROW_SUB, LANES), row),
                     pl.BlockSpec((tm * ROW_SUB, LANES), lambda i: (i + nt, 0)),
                     pl.BlockSpec((tm, LANES), row)]
        args += [ys, ys, gate]
        out_shape.append(jax.ShapeDtypeStruct((T, D_MODEL), F32))
        out_specs.append(pl.BlockSpec((tm, D_MODEL), row))
    in_specs += [pl.BlockSpec((1, D_MODEL), fixed), pl.BlockSpec((D_MODEL, IN_WIDTH), fixed)]
    args += [norm_g.reshape(1, D_MODEL), w_bf]
    out_shape += [jax.ShapeDtypeStruct((T, HG_COLS), F32), jax.ShapeDtypeStruct((T, QKV_COLS), F32),
                  jax.ShapeDtypeStruct((T, S5_WIDTH), F32)]
    out_specs += [pl.BlockSpec((tm, HG_COLS), row), pl.BlockSpec((tm, QKV_COLS), row),
                  pl.BlockSpec((tm, S5_WIDTH), row)]
    return pl.pallas_call(
        functools.partial(_in_proj_body, moe is not None),
        grid=(nt,), in_specs=in_specs, out_specs=out_specs, out_shape=out_shape,
        compiler_params=_params(("parallel",)), name="in_proj")(*args)


def _log_forget_and_key(z, log_lb, log_1m_lb):
    log_sig = jnp.minimum(z, 0.0) - jnp.log1p(jnp.exp(-jnp.abs(z)))
    a = log_1m_lb + log_sig
    log_f = jnp.maximum(a, log_lb) + jnp.log1p(jnp.exp(-jnp.abs(a - log_lb)))
    return log_f, jnp.exp(a - z)


def _split3(x):
    hi = x.astype(BF16)
    r = x - hi.astype(F32)
    mid = r.astype(BF16)
    lo = (r - mid.astype(F32)).astype(BF16)
    return hi, mid, lo


def _hgrn_body(hg_ref, lb_ref, gn_ref, o_ref, of_s, ob_s, st_s, kv_s):
    S = hg_ref.shape[0]
    n_blk = S // HG_BLOCK
    n_sub = HG_BLOCK // HG_CHUNK
    row = lax.broadcasted_iota(jnp.int32, (HG_BLOCK, HG_BLOCK), 0)
    col = lax.broadcasted_iota(jnp.int32, (HG_BLOCK, HG_BLOCK), 1)
    same = (row // HG_CHUNK) == (col // HG_CHUNK)
    lane_head = lax.broadcasted_iota(jnp.int32, (1, HG_WIDTH), 1) // HEAD_DIM
    col_sub = lax.broadcasted_iota(jnp.int32, (HG_WIDTH, HG_BLOCK), 1) // HG_CHUNK
    same_bf = jnp.where(same, 1.0, 0.0).astype(BF16)

    st_s[...] = jnp.zeros(st_s.shape, F32)

    def block(d, j, out_s):
        r0 = pl.multiple_of(j * HG_BLOCK, HG_BLOCK)
        rows = pl.ds(r0, HG_BLOCK)
        tri = same & ((col <= row) if d == 0 else (col >= row))
        z = hg_ref[rows, (3 + d) * HG_WIDTH:(4 + d) * HG_WIDTH]
        lf, k = _log_forget_and_key(z, lb_ref[2 * d:2 * d + 1, :], lb_ref[2 * d + 1:2 * d + 2, :])
        a_mat = jnp.concatenate([jnp.where(tri, 1.0, 0.0).astype(BF16), same_bf], axis=0)
        hi, mid, lo = _split3(lf)
        acc = _dot(a_mat, hi) + _dot(a_mat, mid) + _dot(a_mat, lo)
        b = acc[:HG_BLOCK]
        tot = acc[HG_BLOCK:]
        q = _silu(hg_ref[rows, 0:HG_WIDTH])
        v = hg_ref[rows, HG_WIDTH:2 * HG_WIDTH]
        q_dec = q * jnp.exp(b)
        k_dec = (k * jnp.exp(-b)).astype(BF16)
        k_end = (k * jnp.exp(tot - b)).astype(BF16)
        dec = jnp.exp(tot)
        q_heads = jnp.concatenate(
            [jnp.where(lane_head == h, q_dec, 0.0).astype(BF16) for h in range(HG_HEADS)], axis=0)
        scores = _dot_nt(q_heads, k_dec)
        o_blk = jnp.zeros((HG_BLOCK, HG_WIDTH), F32)
        for h in range(HG_HEADS):
            p_h = jnp.where(tri, scores[h * HG_BLOCK:(h + 1) * HG_BLOCK], 0.0).astype(BF16)
            o_blk += _dot(p_h, jnp.where(lane_head == h, v, 0.0).astype(BF16))
        v_t = v.T
        v_sub = jnp.concatenate(
            [jnp.where(col_sub == c, v_t, 0.0).astype(BF16) for c in range(n_sub)], axis=0)
        kv_s[...] = _dot(v_sub, k_end)
        o_inter = [None] * n_sub
        order = range(n_sub) if d == 0 else range(n_sub - 1, -1, -1)
        for c in order:
            q_c = jnp.concatenate(
                [q_heads[h * HG_BLOCK + c * HG_CHUNK:h * HG_BLOCK + (c + 1) * HG_CHUNK]
                 for h in range(HG_HEADS)], axis=0)
            st = st_s[d]
            o_all = _dot_nt(q_c, st.astype(BF16))
            o_c = jnp.zeros((HG_CHUNK, HG_WIDTH), F32)
            for h in range(HG_HEADS):
                o_c += jnp.where(lane_head == h, o_all[h * HG_CHUNK:(h + 1) * HG_CHUNK], 0.0)
            o_inter[c] = o_c
            st_s[d] = dec[c * HG_CHUNK:c * HG_CHUNK + 1, :] * st + kv_s[c * HG_WIDTH:(c + 1) * HG_WIDTH, :]
        out_s[rows, :] = o_blk + jnp.concatenate(o_inter, axis=0)

    def step(j, carry):
        block(0, j, of_s)
        block(1, n_blk - 1 - j, ob_s)
        return carry

    lax.fori_loop(0, n_blk, step, 0)

    r2 = lax.broadcasted_iota(jnp.int32, (HG_WIDTH, HG_WIDTH), 0) // HEAD_DIM
    c2 = lax.broadcasted_iota(jnp.int32, (HG_WIDTH, HG_WIDTH), 1) // HEAD_DIM
    head_mean = jnp.where(r2 == c2, 1.0 / HEAD_DIM, 0.0).astype(F32)

    def finish(j, carry):
        rows = pl.ds(pl.multiple_of(j * HG_BLOCK, HG_BLOCK), HG_BLOCK)
        o = of_s[rows, :] + ob_s[rows, :]
        ms = _dot_hi(o * o, head_mean)
        y = o * lax.rsqrt(ms + NORM_EPS) * gn_ref[...]
        o_ref[rows, :] = y * _silu(hg_ref[rows, 2 * HG_WIDTH:3 * HG_WIDTH])
        return carry

    lax.fori_loop(0, n_blk, finish, 0)


def _hgrn(hg, lb_rows, gn, B, S):
    return pl.pallas_call(
        _hgrn_body,
        grid=(B,),
        in_specs=[pl.BlockSpec((S, HG_COLS), lambda b: (b, 0)),
                  pl.BlockSpec((4, HG_WIDTH), lambda b: (0, 0)),
                  pl.BlockSpec((1, HG_WIDTH), lambda b: (0, 0))],
        out_specs=pl.BlockSpec((S, HG_WIDTH), lambda b: (b, 0)),
        out_shape=jax.ShapeDtypeStruct((B * S, HG_WIDTH), F32),
        scratch_shapes=[pltpu.VMEM((S, HG_WIDTH), F32), pltpu.VMEM((S, HG_WIDTH), F32),
                        pltpu.VMEM((2, HG_WIDTH, HG_WIDTH), F32),
                        pltpu.VMEM((HG_BLOCK // HG_CHUNK * HG_WIDTH, HG_WIDTH), F32)],
        compiler_params=_params(("parallel",)), name="hgrn2")(hg, lb_rows, gn)


def _attn_body(qkv_ref, cs_ref, gain_ref, o_ref, q_s, k_s, vlo_s, vhi_s):
    S = qkv_ref.shape[0]
    n_qb = S // ATTN_QB
    lane = lax.broadcasted_iota(jnp.int32, (1, LANES), 1)
    lo = lane < HEAD_DIM
    even = (lane % 2) == 0
    r2 = lax.broadcasted_iota(jnp.int32, (LANES, LANES), 0) // HEAD_DIM
    c2 = lax.broadcasted_iota(jnp.int32, (LANES, LANES), 1) // HEAD_DIM
    head_mean = jnp.where(r2 == c2, 1.0 / HEAD_DIM, 0.0).astype(F32)
    n_pairs = ATTN_WIDTH // LANES

    def norm_rope(x, gain, cos, sin):
        ms = _dot_hi(x * x, head_mean)
        y = x * lax.rsqrt(ms + NORM_EPS) * gain
        swapped = jnp.where(even, pltpu.roll(y, LANES - 1, 1), pltpu.roll(y, 1, 1))
        return y * cos + swapped * sin

    def prep(j, carry):
        rows = pl.ds(pl.multiple_of(j * ATTN_QB, ATTN_QB), ATTN_QB)
        cos = cs_ref[0, rows, :]
        sin = cs_ref[1, rows, :]
        for m in range(n_pairs):
            qm = norm_rope(qkv_ref[rows, m * LANES:(m + 1) * LANES], gain_ref[0:1, :], cos, sin)
            qm = qm * (HEAD_DIM ** -0.5)
            q_s[m, 0, rows, :] = jnp.where(lo, qm, 0.0).astype(BF16)
            q_s[m, 1, rows, :] = jnp.where(lo, 0.0, qm).astype(BF16)
        kk = norm_rope(qkv_ref[rows, ATTN_WIDTH:ATTN_WIDTH + LANES], gain_ref[1:2, :], cos, sin)
        kr = pltpu.roll(kk, HEAD_DIM, 1)
        k_s[0, rows, :] = jnp.where(lo, kk, kr).astype(BF16)
        k_s[1, rows, :] = jnp.where(lo, kr, kk).astype(BF16)
        vv = qkv_ref[rows, ATTN_WIDTH + LANES:ATTN_WIDTH + 2 * LANES]
        vr = pltpu.roll(vv, HEAD_DIM, 1)
        ones_hi = jnp.where(lane == HEAD_DIM, 1.0, 0.0)
        ones_lo = jnp.where(lane == 0, 1.0, 0.0)
        vlo_s[0, rows, :] = jnp.where(lo, vv, ones_hi).astype(BF16)
        vhi_s[0, rows, :] = jnp.where(lo, ones_lo, vr).astype(BF16)
        vlo_s[1, rows, :] = jnp.where(lo, vr, ones_hi).astype(BF16)
        vhi_s[1, rows, :] = jnp.where(lo, ones_lo, vv).astype(BF16)
        return carry

    lax.fori_loop(0, n_qb, prep, 0)

    def attend(j, carry):
        rows = pl.ds(pl.multiple_of(j * ATTN_QB, ATTN_QB), ATTN_QB)
        for m in range(n_pairs):
            kv = m // (n_pairs // 2)
            q2 = jnp.concatenate([q_s[m, 0, rows, :], q_s[m, 1, rows, :]], axis=0)
            s = _dot_nt(q2, k_s[kv])
            p = jnp.exp((s - jnp.max(s, axis=-1, keepdims=True)).astype(BF16))
            o_lo = _dot(p[:ATTN_QB], vlo_s[kv])
            o_hi = _dot(p[ATTN_QB:], vhi_s[kv])
            o_ref[rows, m * LANES:(m + 1) * LANES] = jnp.where(
                lo, o_lo / o_lo[:, HEAD_DIM:HEAD_DIM + 1], o_hi / o_hi[:, 0:1])
        return carry

    lax.fori_loop(0, n_qb, attend, 0)


def _attention(qkv, cs, gains, B, S):
    n_pairs = ATTN_WIDTH // LANES
    return pl.pallas_call(
        _attn_body,
        grid=(B,),
        in_specs=[pl.BlockSpec((S, QKV_COLS), lambda b: (b, 0)),
                  pl.BlockSpec((2, S, LANES), lambda b: (0, 0, 0)),
                  pl.BlockSpec((2, LANES), lambda b: (0, 0))],
        out_specs=pl.BlockSpec((S, ATTN_WIDTH), lambda b: (b, 0)),
        out_shape=jax.ShapeDtypeStruct((B * S, ATTN_WIDTH), F32),
        scratch_shapes=[pltpu.VMEM((n_pairs, 2, S, LANES), BF16), pltpu.VMEM((2, S, LANES), BF16),
                        pltpu.VMEM((2, S, LANES), BF16), pltpu.VMEM((2, S, LANES), BF16)],
        compiler_params=_params(("parallel",)), name="gqa_attention")(qkv, cs, gains)


def _rope_tables(S):
    rows_count = S // GRID_W
    rows = jnp.repeat(jnp.arange(rows_count, dtype=F32), GRID_W)
    cols = jnp.tile(jnp.arange(GRID_W, dtype=F32), rows_count)
    pairs = HEAD_DIM // 4
    freqs = jnp.power(jnp.float32(ROPE_THETA), -jnp.arange(pairs, dtype=F32) / pairs)
    ang = jnp.concatenate([rows[:, None] * freqs, cols[:, None] * freqs], axis=-1)
    cos = jnp.repeat(jnp.cos(ang), 2, axis=-1)
    sin = jnp.repeat(jnp.sin(ang), 2, axis=-1) * jnp.tile(jnp.array([-1.0, 1.0], F32), HEAD_DIM // 2)
    reps = LANES // HEAD_DIM
    return jnp.stack([jnp.tile(cos, (1, reps)), jnp.tile(sin, (1, reps))])


def _s5_body(uf_ref, ub_ref, perm_ref, bmat_ref, lam_ref, cmat_ref, yf_ref, yb_ref, st_s, x_s):
    B, lt, _ = uf_ref.shape
    half = S5_REAL // 2

    @pl.when(pl.program_id(0) == 0)
    def _():
        st_s[...] = jnp.zeros(st_s.shape, F32)

    n_tiles = half // LANES

    def direction(d, u_ref, y_ref):
        u = u_ref[...].reshape(B * lt, S5_WIDTH).astype(BF16)
        u_tm = _dot(perm_ref[0], u).astype(BF16)
        bu = _dot(u_tm, bmat_ref[d])
        for c in range(2 * n_tiles):
            x_s[c] = bu[:, c * LANES:(c + 1) * LANES]
        order = range(lt) if d == 0 else range(lt - 1, -1, -1)
        re_t = lambda c: slice(c * LANES, (c + 1) * LANES)
        im_t = lambda c: slice(half + c * LANES, half + (c + 1) * LANES)
        xr = [st_s[d, :, re_t(c)] for c in range(n_tiles)]
        xi = [st_s[d, :, im_t(c)] for c in range(n_tiles)]
        for t in order:
            slab = slice(t * B, (t + 1) * B)
            for c in range(n_tiles):
                lr = lam_ref[d, :, re_t(c)]
                li = lam_ref[d, :, im_t(c)]
                nr = lr * xr[c] - li * xi[c] + x_s[c, slab, :]
                ni = lr * xi[c] + li * xr[c] + x_s[n_tiles + c, slab, :]
                xr[c], xi[c] = nr, ni
                x_s[c, slab, :] = nr
                x_s[n_tiles + c, slab, :] = ni
        for c in range(n_tiles):
            st_s[d, :, re_t(c)] = xr[c]
            st_s[d, :, im_t(c)] = xi[c]
        xs = jnp.concatenate([x_s[c].astype(BF16) for c in range(2 * n_tiles)], axis=1)
        y_tm = _dot(xs, cmat_ref[d])
        hi = y_tm.astype(BF16)
        lo = (y_tm - hi.astype(F32)).astype(BF16)
        y_ref[...] = (_dot(perm_ref[1], hi) + _dot(perm_ref[1], lo)).reshape(B, lt, S5_WIDTH)

    direction(0, uf_ref, yf_ref)
    direction(1, ub_ref, yb_ref)


def _s5_scan(u, bmat, lam_b, cmat, B, S):
    lt = min(S5_LT, S)
    n = S // lt
    fwd = lambda j: (0, j, 0)
    bwd = lambda j: (0, n - 1 - j, 0)
    fixed = lambda j: (0, 0, 0)
    r = jnp.arange(B * lt, dtype=jnp.int32)
    to_time_major = jax.nn.one_hot((r % B) * lt + r // B, B * lt, dtype=BF16)
    perm = jnp.stack([to_time_major, to_time_major.T])
    return pl.pallas_call(
        _s5_body,
        grid=(n,),
        in_specs=[pl.BlockSpec((B, lt, S5_WIDTH), fwd), pl.BlockSpec((B, lt, S5_WIDTH), bwd),
                  pl.BlockSpec((2, B * lt, B * lt), fixed),
                  pl.BlockSpec((2, S5_WIDTH, S5_REAL), fixed), pl.BlockSpec((2, B, S5_REAL), fixed),
                  pl.BlockSpec((2, S5_REAL, S5_WIDTH), fixed)],
        out_specs=[pl.BlockSpec((B, lt, S5_WIDTH), fwd), pl.BlockSpec((B, lt, S5_WIDTH), bwd)],
        out_shape=[jax.ShapeDtypeStruct((B, S, S5_WIDTH), F32)] * 2,
        scratch_shapes=[pltpu.VMEM((2, B, S5_REAL), F32), pltpu.VMEM((S5_REAL // LANES, B * lt, LANES), F32)],
        compiler_params=_params(("arbitrary",)), name="s5_scan")(u, u, perm, bmat, lam_b, cmat)


def _s5_operators(a_re, a_im, log_dt, b_re, b_im, c_re, c_im, B):
    dt = jnp.exp(log_dt.astype(F32))[..., None]
    ar, ai = a_re.astype(F32), a_im.astype(F32)
    mag = jnp.exp(ar * dt)
    lr, li = mag * jnp.cos(ai * dt), mag * jnp.sin(ai * dt)
    den = ar * ar + ai * ai
    fr = ((lr - 1.0) * ar + li * ai) / den
    fi = (li * ar - (lr - 1.0) * ai) / den
    br, bi = b_re.astype(F32), b_im.astype(F32)
    bbr = fr[..., None] * br - fi[..., None] * bi
    bbi = fr[..., None] * bi + fi[..., None] * br
    eye = jnp.eye(S5_GROUPS, dtype=F32)
    in_blk = lambda m: jnp.einsum('zgpc,gh->zgchp', m, eye).reshape(2, S5_WIDTH, S5_REAL // 2)
    out_blk = lambda m: jnp.einsum('zgcp,gh->zgphc', m, eye).reshape(2, S5_REAL // 2, S5_WIDTH)
    bmat = jnp.concatenate([in_blk(bbr), in_blk(bbi)], axis=-1).astype(BF16)
    cmat = jnp.concatenate([out_blk(c_re.astype(F32)), -out_blk(c_im.astype(F32))], axis=1).astype(BF16)
    lam = jnp.concatenate([lr.reshape(2, 1, -1), li.reshape(2, 1, -1)], axis=-1)
    return bmat, jnp.broadcast_to(lam, (2, B, S5_REAL)), cmat


def _gelu_tanh(x):
    return 0.5 * x * (1.0 + jnp.tanh(math.sqrt(2.0 / math.pi) * (x + 0.044715 * (x * x * x))))


def _out_proj_body(x_ref, oa_ref, ob_ref, u_ref, yf_ref, yb_ref, dskip_ref, wglu_ref, bglu_ref, wout_ref,
                   g2_ref, wr_ref, br_ref, x1_ref, h2_ref, eid_ref, gate_ref):
    y = dskip_ref[...] * u_ref[...] + yf_ref[...] + yb_ref[...]
    z = _gelu_tanh(y)
    t = _dot(z.astype(BF16), wglu_ref[...]) + bglu_ref[...]
    oc = z * (1.0 / (1.0 + jnp.exp(-t)))
    mix = _dot(oa_ref[...].astype(BF16), wout_ref[0:HG_WIDTH, :])
    mix += _dot(ob_ref[...].astype(BF16), wout_ref[HG_WIDTH:HG_WIDTH + ATTN_WIDTH, :])
    mix += _dot(oc.astype(BF16), wout_ref[HG_WIDTH + ATTN_WIDTH:, :])
    x1 = x_ref[...] + mix
    x1_ref[...] = x1
    h2 = x1 * lax.rsqrt(jnp.mean(x1 * x1, axis=-1, keepdims=True) + NORM_EPS) * g2_ref[...]
    _store_rows(h2_ref, h2)
    h_hi = h2.astype(BF16)
    h_lo = (h2 - h_hi.astype(F32)).astype(BF16)
    logits = _dot(h_hi, wr_ref[0]) + (_dot(h_lo, wr_ref[0]) + _dot(h_hi, wr_ref[1])) + br_ref[...]
    lane = lax.broadcasted_iota(jnp.int32, logits.shape, 1)
    lane_f = lane.astype(F32)
    neg = jnp.float32(-jnp.inf)
    is_g = lane < N_GROUPS
    gl = jnp.where(is_g, logits, neg)
    gmax = jnp.max(gl, axis=-1, keepdims=True)
    gsel = jnp.min(jnp.where(gl == gmax, lane_f, float(ROUTER_LANES)), axis=-1, keepdims=True)
    p_group = 1.0 / jnp.sum(jnp.where(is_g, jnp.exp(logits - gmax), 0.0), axis=-1, keepdims=True)
    first = N_GROUPS + PER_GROUP * gsel
    el = jnp.where((lane_f >= first) & (lane_f < first + PER_GROUP), logits, neg)
    v1 = jnp.max(el, axis=-1, keepdims=True)
    i1 = jnp.min(jnp.where(el == v1, lane_f, float(ROUTER_LANES)), axis=-1, keepdims=True)
    el2 = jnp.where(lane_f == i1, neg, el)
    v2 = jnp.max(el2, axis=-1, keepdims=True)
    i2 = jnp.min(jnp.where(el2 == v2, lane_f, float(ROUTER_LANES)), axis=-1, keepdims=True)
    e2 = jnp.exp(v2 - v1)
    w1 = p_group / (1.0 + e2)
    w2 = p_group * e2 / (1.0 + e2)
    eid_ref[...] = jnp.where(lane == 0, i1 - N_GROUPS, jnp.where(lane == 1, i2 - N_GROUPS, 0.0)).astype(jnp.int32)
    gate_ref[...] = jnp.where(lane == 0, w1, jnp.where(lane == 1, w2, 0.0))


def _out_proj(x, oa, ob, u, yf, yb, dskip, wglu_bf, bglu, wout_bf, g2, wr, br):
    T = x.shape[0]
    tm = min(ROW_TILE, T)
    row = lambda i: (i, 0)
    fixed = lambda i: (0, 0)
    blk = lambda w: pl.BlockSpec((tm, w), row)
    full = lambda a: pl.BlockSpec(a.shape, lambda i: (0,) * a.ndim)
    consts = [dskip, wglu_bf, bglu, wout_bf, g2, wr, br]
    return pl.pallas_call(
        _out_proj_body,
        grid=(T // tm,),
        in_specs=[blk(D_MODEL), blk(HG_WIDTH), blk(ATTN_WIDTH), blk(S5_WIDTH), blk(S5_WIDTH), blk(S5_WIDTH)]
        + [full(a) for a in consts],
        out_specs=[blk(D_MODEL), pl.BlockSpec((tm * ROW_SUB, LANES), row), blk(ROUTER_LANES), blk(ROUTER_LANES)],
        out_shape=[jax.ShapeDtypeStruct((T, D_MODEL), F32), jax.ShapeDtypeStruct((T * ROW_SUB, LANES), F32),
                   jax.ShapeDtypeStruct((T, ROUTER_LANES), jnp.int32),
                   jax.ShapeDtypeStruct((T, ROUTER_LANES), F32)],
        compiler_params=_params(("parallel",)), name="out_proj_router")(x, oa, ob, u, yf, yb, *consts)


def _moe_body(bm, blk_e_ref, n_used_ref, cnt_ref, tok_ref, tok_next_ref, dst_ref, h2_hbm, wg_ref, wu_ref, wd_ref,
              out_hbm, xbuf, obuf, wg_s, wu_s, wd_s, gsem, ssem):
    i = pl.program_id(0)
    n_used = n_used_ref[0]
    slot = i % 2

    @pl.when((i == 0) | (blk_e_ref[i] != blk_e_ref[jnp.maximum(i - 1, 0)]))
    def _():
        wg_s[...] = wg_ref[...].astype(BF16)
        wu_s[...] = wu_ref[...].astype(BF16)
        wd_s[...] = wd_ref[...].astype(BF16)

    def gather(idx_ref, s):
        def issue(r, carry):
            pltpu.make_async_copy(h2_hbm.at[idx_ref[0, 0, r]],
                                  xbuf.at[s, pl.ds(pl.multiple_of(r * ROW_SUB, ROW_SUB), ROW_SUB)],
                                  gsem.at[s]).start()
            return carry
        lax.fori_loop(0, bm, issue, 0, unroll=8)

    def scatter_wait(s, n):
        @pl.when(n > 0)
        def _():
            rows = pl.ds(0, n * ROW_SUB)
            pltpu.make_async_copy(obuf.at[s, rows], obuf.at[s, rows], ssem.at[s]).wait()

    @pl.when(i == 0)
    def _():
        gather(tok_ref, 0)

    @pl.when(i + 1 < n_used)
    def _():
        gather(tok_next_ref, 1 - slot)

    @pl.when(i < n_used)
    def _():
        pltpu.make_async_copy(xbuf.at[slot], xbuf.at[slot], gsem.at[slot]).wait()
        xb = _load_rows(xbuf.at[slot], bm).astype(BF16)
        g = _dot(xb, wg_s[...])
        u = _dot(xb, wu_s[...])
        _store_rows(obuf.at[slot], _dot((_silu(g) * u).astype(BF16), wd_s[...]))
        cnt = cnt_ref[i]

        def issue(r, carry):
            pltpu.make_async_copy(obuf.at[slot, pl.ds(pl.multiple_of(r * ROW_SUB, ROW_SUB), ROW_SUB)],
                                  out_hbm.at[dst_ref[0, 0, r]], ssem.at[slot]).start()
            return carry
        lax.fori_loop(0, cnt, issue, 0)

        @pl.when(i > 0)
        def _():
            scatter_wait(1 - slot, cnt_ref[i - 1])

        @pl.when(i == n_used - 1)
        def _():
            scatter_wait(slot, cnt)


def _moe_experts(h2, tables, wg_bf, wu_bf, wd_bf, layer, bm, T):
    row_tok, row_dst, blk_e, blk_cnt, n_used = tables
    n_blocks = blk_e.shape[0]
    tok3 = row_tok.reshape(n_blocks, 1, bm)
    dst3 = row_dst.reshape(n_blocks, 1, bm)
    smem_blk = lambda f: pl.BlockSpec((1, 1, bm), f, memory_space=pltpu.SMEM)
    wspec = lambda shape: pl.BlockSpec((None, None) + shape, lambda i, be, nu, cn: (layer, be[i], 0, 0))
    out = pl.pallas_call(
        functools.partial(_moe_body, bm),
        grid_spec=pltpu.PrefetchScalarGridSpec(
            num_scalar_prefetch=3, grid=(n_blocks,),
            in_specs=[smem_blk(lambda i, be, nu, cn: (i, 0, 0)),
                      smem_blk(lambda i, be, nu, cn: (jnp.minimum(i + 1, n_blocks - 1), 0, 0)),
                      smem_blk(lambda i, be, nu, cn: (i, 0, 0)),
                      pl.BlockSpec(memory_space=pl.ANY),
                      wspec((D_MODEL, EXPERT_FF)), wspec((D_MODEL, EXPERT_FF)), wspec((EXPERT_FF, D_MODEL))],
            out_specs=pl.BlockSpec(memory_space=pl.ANY),
            scratch_shapes=[pltpu.VMEM((2, bm * ROW_SUB, LANES), F32), pltpu.VMEM((2, bm * ROW_SUB, LANES), F32),
                            pltpu.VMEM((D_MODEL, EXPERT_FF), BF16), pltpu.VMEM((D_MODEL, EXPERT_FF), BF16),
                            pltpu.VMEM((EXPERT_FF, D_MODEL), BF16),
                            pltpu.SemaphoreType.DMA((2,)), pltpu.SemaphoreType.DMA((2,))]),
        out_shape=jax.ShapeDtypeStruct((TOP_K * T, ROW_SUB, LANES), F32),
        compiler_params=_params(("arbitrary",)), name="moe_experts")(
            blk_e, n_used, blk_cnt, tok3, tok3, dst3, h2.reshape(T, ROW_SUB, LANES), wg_bf, wu_bf, wd_bf)
    return out.reshape(TOP_K * T * ROW_SUB, LANES)


def _dispatch_tables(eid, T, bm):
    n_assign = T * TOP_K
    flat_e = eid.reshape(n_assign)
    onehot = (flat_e[:, None] == jnp.arange(N_EXPERTS, dtype=jnp.int32)[None, :]).astype(jnp.int32)
    csum = jnp.cumsum(onehot, axis=0)
    rank = jnp.sum(csum * onehot, axis=1) - 1
    counts = csum[-1]
    padded = ((counts + bm - 1) // bm) * bm
    pends = jnp.cumsum(padded)
    pstarts = pends - padded
    dest = (jnp.sum(onehot * pstarts[None, :], axis=1) + rank).astype(jnp.int32)
    n_blocks = n_assign // bm + N_EXPERTS
    assign = jnp.arange(n_assign, dtype=jnp.int32)
    row_assign = jnp.zeros((n_blocks * bm,), jnp.int32).at[dest].set(assign)
    row_tok = row_assign // TOP_K
    row_dst = (row_assign % TOP_K) * T + row_tok
    blk_start = jnp.arange(n_blocks, dtype=jnp.int32) * bm
    blk_e = jnp.sum((blk_start[:, None] >= pends[None, :]).astype(jnp.int32), axis=1)
    blk_e = jnp.minimum(blk_e, N_EXPERTS - 1)
    blk_cnt = jnp.clip(counts[blk_e] - (blk_start - pstarts[blk_e]), 0, bm).astype(jnp.int32)
    n_used = (pends[-1] // bm).astype(jnp.int32).reshape(1)
    return row_tok, row_dst, blk_e, blk_cnt, n_used


def _moe(h2, eid, wg_bf, wu_bf, wd_bf, layer):
    T = h2.shape[0] // ROW_SUB
    bm = min(MOE_BM, T)
    return _moe_experts(h2, _dispatch_tables(eid, T, bm), wg_bf, wu_bf, wd_bf, layer, bm, T)


def _final_body(x_ref, y0_ref, y1_ref, gate_ref, g_ref, o_ref):
    gate = gate_ref[...]
    n = x_ref.shape[0]
    x = x_ref[...] + gate[:, 0:1] * _load_rows(y0_ref, n) + gate[:, 1:2] * _load_rows(y1_ref, n)
    o_ref[...] = x * lax.rsqrt(jnp.mean(x * x, axis=-1, keepdims=True) + NORM_EPS) * g_ref[...]


def _final_norm(x, ys, gate, g):
    T = x.shape[0]
    tm = min(ROW_TILE, T)
    nt = T // tm
    row = lambda i: (i, 0)
    return pl.pallas_call(
        _final_body,
        grid=(nt,),
        in_specs=[pl.BlockSpec((tm, D_MODEL), row), pl.BlockSpec((tm * ROW_SUB, LANES), row),
                  pl.BlockSpec((tm * ROW_SUB, LANES), lambda i: (i + nt, 0)), pl.BlockSpec((tm, LANES), row),
                  pl.BlockSpec((1, D_MODEL), lambda i: (0, 0))],
        out_specs=pl.BlockSpec((tm, D_MODEL), row),
        out_shape=jax.ShapeDtypeStruct((T, D_MODEL), F32),
        compiler_params=_params(("parallel",)), name="combine_final_norm")(x, ys, ys, gate, g.reshape(1, D_MODEL))


def kernel(x, norm1_g, w_in, hgrn_lower_bounds, hgrn_norm_g, attn_q_norm_g, attn_k_norm_g, s5_a_re, s5_a_im, s5_log_dt, s5_b_re, s5_b_im, s5_c_re, s5_c_im, s5_d, s5_w_glu, s5_b_glu, w_out, norm2_g, router_group_w, router_group_b, router_expert_w, router_expert_b, expert_w_gate, expert_w_up, expert_w_down, final_norm_g):
    B, S, D = x.shape
    T = B * S
    depth = w_in.shape[0]
    cs = _rope_tables(S)
    lb_all = jnp.cumsum(jax.nn.softmax(hgrn_lower_bounds.astype(F32), axis=0), axis=0)
    lb_all = lb_all - lb_all[0:1]
    wg_bf, wu_bf, wd_bf = expert_w_gate, expert_w_up, expert_w_down
    reps = LANES // HEAD_DIM
    pad = ROUTER_LANES - N_GROUPS - N_EXPERTS

    xt = x.reshape(T, D)
    moe = None
    for l in range(depth):
        if moe is None:
            hg, qkv, u = _in_proj(xt, norm1_g[l], w_in[l].astype(BF16))
        else:
            xt, hg, qkv, u = _in_proj(xt, norm1_g[l], w_in[l].astype(BF16), moe)
        lb = lb_all[l]
        lb_rows = jnp.stack([jnp.log(lb[0]), jnp.log1p(-lb[0]), jnp.log(lb[1]), jnp.log1p(-lb[1])])
        oa = _hgrn(hg, lb_rows, jnp.tile(hgrn_norm_g[l].astype(F32), HG_HEADS).reshape(1, HG_WIDTH), B, S)
        gains = jnp.stack([jnp.tile(attn_q_norm_g[l].astype(F32), reps), jnp.tile(attn_k_norm_g[l].astype(F32), reps)])
        ob = _attention(qkv, cs, gains, B, S)
        bmat, lam_b, cmat = _s5_operators(s5_a_re[l], s5_a_im[l], s5_log_dt[l], s5_b_re[l], s5_b_im[l],
                                          s5_c_re[l], s5_c_im[l], B)
        yf, yb = _s5_scan(u.reshape(B, S, S5_WIDTH), bmat, lam_b, cmat, B, S)
        wr = jnp.concatenate([router_group_w[l].astype(F32), router_expert_w[l].astype(F32),
                              jnp.zeros((D, pad), F32)], axis=1)
        wr_hi = wr.astype(BF16)
        wr = jnp.stack([wr_hi, (wr - wr_hi.astype(F32)).astype(BF16)])
        br =jnp.concatenate([router_group_b[l].astype(F32), router_expert_b[l].astype(F32),
                              jnp.zeros((pad,), F32)]).reshape(1, ROUTER_LANES)
        xt, h2, eid, gate = _out_proj(
            xt, oa, ob, u, yf.reshape(T, S5_WIDTH), yb.reshape(T, S5_WIDTH),
            s5_d[l].astype(F32).reshape(1, S5_WIDTH), s5_w_glu[l].astype(BF16),
            s5_b_glu[l].astype(F32).reshape(1, S5_WIDTH), w_out[l].astype(BF16),
            norm2_g[l].astype(F32).reshape(1, D), wr, br)
        moe = (_moe(h2, eid[:, :TOP_K], wg_bf, wu_bf, wd_bf, l), gate)
    out = _final_norm(xt, moe[0], moe[1], final_norm_g.astype(F32))
    return out.reshape(B, S, D)
```

```python
import functools
import math

import jax
import jax.numpy as jnp
from jax import lax
from jax.experimental import pallas as pl
from jax.experimental.pallas import tpu as pltpu

F32 = jnp.float32
BF16 = jnp.bfloat16
HIGHEST = lax.Precision.HIGHEST

D_MODEL = 1024
HEAD_DIM = 64
HG_WIDTH = 256
S5_WIDTH = 256
ATTN_WIDTH = 512
KV_WIDTH = 128
HG_HEADS = HG_WIDTH // HEAD_DIM
HG_CHUNK = 16
ROPE_THETA = 10000.0
GRID_W = 64
S5_CH = 16
S5_GROUPS = S5_WIDTH // S5_CH
S5_STATE = 64
N_GROUPS = 4
PER_GROUP = 8
N_EXPERTS = N_GROUPS * PER_GROUP
TOP_K = 2
EXPERT_FF = D_MODEL // 2
NORM_EPS = 1e-6
HG_COLS = 5 * HG_WIDTH
QKV_COLS = ATTN_WIDTH + 2 * KV_WIDTH
IN_WIDTH = HG_COLS + QKV_COLS + S5_WIDTH
S5_REAL = 2 * S5_GROUPS * S5_STATE

LANES = 128
HG_BLOCK = 128
ATTN_QB = 256
S5_LT = 32
ROW_TILE = 512
MOE_BM = 256
ROUTER_LANES = 128
VMEM_LIMIT = 56 * 1024 * 1024


def _dot(a, b):
    return jnp.dot(a, b, preferred_element_type=F32)


def _dot_hi(a, b):
    return jnp.dot(a, b, preferred_element_type=F32, precision=HIGHEST)


def _dot_nt(a, b):
    return lax.dot_general(a, b, (((1,), (1,)), ((), ())), preferred_element_type=F32)


def _silu(x):
    return x * (1.0 / (1.0 + jnp.exp(-x)))


def _params(sem):
    return pltpu.CompilerParams(dimension_semantics=sem, vmem_limit_bytes=VMEM_LIMIT)


ROW_SUB = D_MODEL // LANES


def _load_rows(ref, n):
    return jnp.concatenate([ref[pl.ds(j, n, stride=ROW_SUB), :] for j in range(ROW_SUB)], axis=1)


def _store_rows(ref, val):
    n = val.shape[0]
    for j in range(ROW_SUB):
        ref[pl.ds(j, n, stride=ROW_SUB), :] = val[:, j * LANES:(j + 1) * LANES]


def _in_proj_body(combine, *refs):
    if combine:
        x_ref, y0_ref, y1_ref, gate_ref, g_ref, w_ref, xo_ref, hg_ref, qkv_ref, u_ref = refs
        gate = gate_ref[...]
        n = x_ref.shape[0]
        x = x_ref[...] + gate[:, 0:1] * _load_rows(y0_ref, n) + gate[:, 1:2] * _load_rows(y1_ref, n)
        xo_ref[...] = x
    else:
        x_ref, g_ref, w_ref, hg_ref, qkv_ref, u_ref = refs
        x = x_ref[...]
    h = x * lax.rsqrt(jnp.mean(x * x, axis=-1, keepdims=True) + NORM_EPS) * g_ref[...]
    hb = h.astype(BF16)
    hg_ref[...] = _dot(hb, w_ref[:, :HG_COLS])
    qkv_ref[...] = _dot(hb, w_ref[:, HG_COLS:HG_COLS + QKV_COLS])
    u_ref[...] = _dot(hb, w_ref[:, HG_COLS + QKV_COLS:])


def _in_proj(x, norm_g, w_bf, moe=None):
    T = x.shape[0]
    tm = min(ROW_TILE, T)
    nt = T // tm
    row = lambda i: (i, 0)
    fixed = lambda i: (0, 0)
    in_specs = [pl.BlockSpec((tm, D_MODEL), row)]
    args = [x]
    out_shape = []
    out_specs = []
    if moe is not None:
        ys, gate = moe
        in_specs += [pl.BlockSpec((tm * ROW_SUB, LANES), row),
                     pl.BlockSpec((tm * ROW_SUB, LANES), lambda i: (i + nt, 0)),
                     pl.BlockSpec((tm, LANES), row)]
        args += [ys, ys, gate]
        out_shape.append(jax.ShapeDtypeStruct((T, D_MODEL), F32))
        out_specs.append(pl.BlockSpec((tm, D_MODEL), row))
    in_specs += [pl.BlockSpec((1, D_MODEL), fixed), pl.BlockSpec((D_MODEL, IN_WIDTH), fixed)]
    args += [norm_g.reshape(1, D_MODEL), w_bf]
    out_shape += [jax.ShapeDtypeStruct((T, HG_COLS), F32), jax.ShapeDtypeStruct((T, QKV_COLS), F32),
                  jax.ShapeDtypeStruct((T, S5_WIDTH), F32)]
    out_specs += [pl.BlockSpec((tm, HG_COLS), row), pl.BlockSpec((tm, QKV_COLS), row),
                  pl.BlockSpec((tm, S5_WIDTH), row)]
    return pl.pallas_call(
        functools.partial(_in_proj_body, moe is not None),
        grid=(nt,), in_specs=in_specs, out_specs=out_specs, out_shape=out_shape,
        compiler_params=_params(("parallel",)), name="in_proj")(*args)


def _log_forget_and_key(z, log_lb, log_1m_lb):
    log_sig = jnp.minimum(z, 0.0) - jnp.log1p(jnp.exp(-jnp.abs(z)))
    a = log_1m_lb + log_sig
    log_f = jnp.maximum(a, log_lb) + jnp.log1p(jnp.exp(-jnp.abs(a - log_lb)))
    return log_f, jnp.exp(a - z)


def _split3(x):
    hi = x.astype(BF16)
    r = x - hi.astype(F32)
    mid = r.astype(BF16)
    lo = (r - mid.astype(F32)).astype(BF16)
    return hi, mid, lo


def _hgrn_body(hg_ref, lb_ref, gn_ref, o_ref, of_s, ob_s, st_s, kv_s):
    S = hg_ref.shape[0]
    n_blk = S // HG_BLOCK
    n_sub = HG_BLOCK // HG_CHUNK
    row = lax.broadcasted_iota(jnp.int32, (HG_BLOCK, HG_BLOCK), 0)
    col = lax.broadcasted_iota(jnp.int32, (HG_BLOCK, HG_BLOCK), 1)
    same = (row // HG_CHUNK) == (col // HG_CHUNK)
    lane_head = lax.broadcasted_iota(jnp.int32, (1, HG_WIDTH), 1) // HEAD_DIM
    col_sub = lax.broadcasted_iota(jnp.int32, (HG_WIDTH, HG_BLOCK), 1) // HG_CHUNK
    same_bf = jnp.where(same, 1.0, 0.0).astype(BF16)

    st_s[...] = jnp.zeros(st_s.shape, F32)

    def block(d, j, out_s):
        r0 = pl.multiple_of(j * HG_BLOCK, HG_BLOCK)
        rows = pl.ds(r0, HG_BLOCK)
        tri = same & ((col <= row) if d == 0 else (col >= row))
        z = hg_ref[rows, (3 + d) * HG_WIDTH:(4 + d) * HG_WIDTH]
        lf, k = _log_forget_and_key(z, lb_ref[2 * d:2 * d + 1, :], lb_ref[2 * d + 1:2 * d + 2, :])
        a_mat = jnp.concatenate([jnp.where(tri, 1.0, 0.0).astype(BF16), same_bf], axis=0)
        hi, mid, lo = _split3(lf)
        acc = _dot(a_mat, hi) + _dot(a_mat, mid) + _dot(a_mat, lo)
        b = acc[:HG_BLOCK]
        tot = acc[HG_BLOCK:]
        q = _silu(hg_ref[rows, 0:HG_WIDTH])
        v = hg_ref[rows, HG_WIDTH:2 * HG_WIDTH]
        q_dec = q * jnp.exp(b)
        k_dec = (k * jnp.exp(-b)).astype(BF16)
        k_end = (k * jnp.exp(tot - b)).astype(BF16)
        dec = jnp.exp(tot)
        q_heads = jnp.concatenate(
            [jnp.where(lane_head == h, q_dec, 0.0).astype(BF16) for h in range(HG_HEADS)], axis=0)
        scores = _dot_nt(q_heads, k_dec)
        o_blk = jnp.zeros((HG_BLOCK, HG_WIDTH), F32)
        for h in range(HG_HEADS):
            p_h = jnp.where(tri, scores[h * HG_BLOCK:(h + 1) * HG_BLOCK], 0.0).astype(BF16)
            o_blk += _dot(p_h, jnp.where(lane_head == h, v, 0.0).astype(BF16))
        v_t = v.T
        v_sub = jnp.concatenate(
            [jnp.where(col_sub == c, v_t, 0.0).astype(BF16) for c in range(n_sub)], axis=0)
        kv_s[d] = _dot(v_sub, k_end)
        o_inter = [None] * n_sub
        order = range(n_sub) if d == 0 else range(n_sub - 1, -1, -1)
        for c in order:
            q_c = jnp.concatenate(
                [q_heads[h * HG_BLOCK + c * HG_CHUNK:h * HG_BLOCK + (c + 1) * HG_CHUNK]
                 for h in range(HG_HEADS)], axis=0)
            st = st_s[d]
            o_all = _dot_nt(q_c, st.astype(BF16))
            o_c = jnp.zeros((HG_CHUNK, HG_WIDTH), F32)
            for h in range(HG_HEADS):
                o_c += jnp.where(lane_head == h, o_all[h * HG_CHUNK:(h + 1) * HG_CHUNK], 0.0)
            o_inter[c] = o_c
            st_s[d] = dec[c * HG_CHUNK:c * HG_CHUNK + 1, :] * st + kv_s[d, c * HG_WIDTH:(c + 1) * HG_WIDTH, :]
        out_s[rows, :] = o_blk + jnp.concatenate(o_inter, axis=0)

    def step(j, carry):
        block(0, j, of_s)
        block(1, n_blk - 1 - j, ob_s)
        return carry

    lax.fori_loop(0, n_blk, step, 0)

    r2 = lax.broadcasted_iota(jnp.int32, (HG_WIDTH, HG_WIDTH), 0) // HEAD_DIM
    c2 = lax.broadcasted_iota(jnp.int32, (HG_WIDTH, HG_WIDTH), 1) // HEAD_DIM
    head_mean = jnp.where(r2 == c2, 1.0 / HEAD_DIM, 0.0).astype(F32)

    def finish(j, carry):
        rows = pl.ds(pl.multiple_of(j * HG_BLOCK, HG_BLOCK), HG_BLOCK)
        o = of_s[rows, :] + ob_s[rows, :]
        ms = _dot_hi(o * o, head_mean)
        y = o * lax.rsqrt(ms + NORM_EPS) * gn_ref[...]
        o_ref[rows, :] = y * _silu(hg_ref[rows, 2 * HG_WIDTH:3 * HG_WIDTH])
        return carry

    lax.fori_loop(0, n_blk, finish, 0)


def _hgrn(hg, lb_rows, gn, B, S):
    return pl.pallas_call(
        _hgrn_body,
        grid=(B,),
        in_specs=[pl.BlockSpec((S, HG_COLS), lambda b: (b, 0)),
                  pl.BlockSpec((4, HG_WIDTH), lambda b: (0, 0)),
                  pl.BlockSpec((1, HG_WIDTH), lambda b: (0, 0))],
        out_specs=pl.BlockSpec((S, HG_WIDTH), lambda b: (b, 0)),
        out_shape=jax.ShapeDtypeStruct((B * S, HG_WIDTH), F32),
        scratch_shapes=[pltpu.VMEM((S, HG_WIDTH), F32), pltpu.VMEM((S, HG_WIDTH), F32),
                        pltpu.VMEM((2, HG_WIDTH, HG_WIDTH), F32),
                        pltpu.VMEM((2, HG_BLOCK // HG_CHUNK * HG_WIDTH, HG_WIDTH), F32)],
        compiler_params=_params(("parallel",)), name="hgrn2")(hg, lb_rows, gn)


def _attn_body(qkv_ref, cs_ref, gain_ref, o_ref, q_s, k_s, vlo_s, vhi_s):
    S = qkv_ref.shape[0]
    n_qb = S // ATTN_QB
    lane = lax.broadcasted_iota(jnp.int32, (1, LANES), 1)
    lo = lane < HEAD_DIM
    even = (lane % 2) == 0
    r2 = lax.broadcasted_iota(jnp.int32, (LANES, LANES), 0) // HEAD_DIM
    c2 = lax.broadcasted_iota(jnp.int32, (LANES, LANES), 1) // HEAD_DIM
    head_mean = jnp.where(r2 == c2, 1.0 / HEAD_DIM, 0.0).astype(F32)
    n_pairs = ATTN_WIDTH // LANES

    def norm_rope(x, gain, cos, sin):
        ms = _dot_hi(x * x, head_mean)
        y = x * lax.rsqrt(ms + NORM_EPS) * gain
        swapped = jnp.where(even, pltpu.roll(y, LANES - 1, 1), pltpu.roll(y, 1, 1))
        return y * cos + swapped * sin

    def prep(j, carry):
        rows = pl.ds(pl.multiple_of(j * ATTN_QB, ATTN_QB), ATTN_QB)
        cos = cs_ref[0, rows, :]
        sin = cs_ref[1, rows, :]
        for m in range(n_pairs):
            qm = norm_rope(qkv_ref[rows, m * LANES:(m + 1) * LANES], gain_ref[0:1, :], cos, sin)
            qm = qm * (HEAD_DIM ** -0.5)
            q_s[m, 0, rows, :] = jnp.where(lo, qm, 0.0).astype(BF16)
            q_s[m, 1, rows, :] = jnp.where(lo, 0.0, qm).astype(BF16)
        kk = norm_rope(qkv_ref[rows, ATTN_WIDTH:ATTN_WIDTH + LANES], gain_ref[1:2, :], cos, sin)
        kr = pltpu.roll(kk, HEAD_DIM, 1)
        k_s[0, rows, :] = jnp.where(lo, kk, kr).astype(BF16)
        k_s[1, rows, :] = jnp.where(lo, kr, kk).astype(BF16)
        vv = qkv_ref[rows, ATTN_WIDTH + LANES:ATTN_WIDTH + 2 * LANES]
        vr = pltpu.roll(vv, HEAD_DIM, 1)
        ones_hi = jnp.where(lane == HEAD_DIM, 1.0, 0.0)
        ones_lo = jnp.where(lane == 0, 1.0, 0.0)
        vlo_s[0, rows, :] = jnp.where(lo, vv, ones_hi).astype(BF16)
        vhi_s[0, rows, :] = jnp.where(lo, ones_lo, vr).astype(BF16)
        vlo_s[1, rows, :] = jnp.where(lo, vr, ones_hi).astype(BF16)
        vhi_s[1, rows, :] = jnp.where(lo, ones_lo, vv).astype(BF16)
        return carry

    lax.fori_loop(0, n_qb, prep, 0)

    def attend(j, carry):
        rows = pl.ds(pl.multiple_of(j * ATTN_QB, ATTN_QB), ATTN_QB)
        for m in range(n_pairs):
            kv = m // (n_pairs // 2)
            q2 = jnp.concatenate([q_s[m, 0, rows, :], q_s[m, 1, rows, :]], axis=0)
            s = _dot_nt(q2, k_s[kv])
            p = jnp.exp((s - jnp.max(s, axis=-1, keepdims=True)).astype(BF16))
            o_lo = _dot(p[:ATTN_QB], vlo_s[kv])
            o_hi = _dot(p[ATTN_QB:], vhi_s[kv])
            o_ref[rows, m * LANES:(m + 1) * LANES] = jnp.where(
                lo, o_lo / o_lo[:, HEAD_DIM:HEAD_DIM + 1], o_hi / o_hi[:, 0:1])
        return carry

    lax.fori_loop(0, n_qb, attend, 0)


def _attention(qkv, cs, gains, B, S):
    n_pairs = ATTN_WIDTH // LANES
    return pl.pallas_call(
        _attn_body,
        grid=(B,),
        in_specs=[pl.BlockSpec((S, QKV_COLS), lambda b: (b, 0)),
                  pl.BlockSpec((2, S, LANES), lambda b: (0, 0, 0)),
                  pl.BlockSpec((2, LANES), lambda b: (0, 0))],
        out_specs=pl.BlockSpec((S, ATTN_WIDTH), lambda b: (b, 0)),
        out_shape=jax.ShapeDtypeStruct((B * S, ATTN_WIDTH), F32),
        scratch_shapes=[pltpu.VMEM((n_pairs, 2, S, LANES), BF16), pltpu.VMEM((2, S, LANES), BF16),
                        pltpu.VMEM((2, S, LANES), BF16), pltpu.VMEM((2, S, LANES), BF16)],
        compiler_params=_params(("parallel",)), name="gqa_attention")(qkv, cs, gains)


def _rope_tables(S):
    rows_count = S // GRID_W
    rows = jnp.repeat(jnp.arange(rows_count, dtype=F32), GRID_W)
    cols = jnp.tile(jnp.arange(GRID_W, dtype=F32), rows_count)
    pairs = HEAD_DIM // 4
    freqs = jnp.power(jnp.float32(ROPE_THETA), -jnp.arange(pairs, dtype=F32) / pairs)
    ang = jnp.concatenate([rows[:, None] * freqs, cols[:, None] * freqs], axis=-1)
    cos = jnp.repeat(jnp.cos(ang), 2, axis=-1)
    sin = jnp.repeat(jnp.sin(ang), 2, axis=-1) * jnp.tile(jnp.array([-1.0, 1.0], F32), HEAD_DIM // 2)
    reps = LANES // HEAD_DIM
    return jnp.stack([jnp.tile(cos, (1, reps)), jnp.tile(sin, (1, reps))])


def _s5_body(uf_ref, ub_ref, perm_ref, bmat_ref, lam_ref, cmat_ref, yf_ref, yb_ref, st_s, x_s):
    B, lt, _ = uf_ref.shape
    half = S5_REAL // 2

    @pl.when(pl.program_id(0) == 0)
    def _():
        st_s[...] = jnp.zeros(st_s.shape, F32)

    n_tiles = half // LANES

    def direction(d, u_ref, y_ref):
        u = u_ref[...].reshape(B * lt, S5_WIDTH).astype(BF16)
        u_tm = _dot(perm_ref[0], u).astype(BF16)
        bu = _dot(u_tm, bmat_ref[d])
        for c in range(2 * n_tiles):
            x_s[c] = bu[:, c * LANES:(c + 1) * LANES]
        order = range(lt) if d == 0 else range(lt - 1, -1, -1)
        re_t = lambda c: slice(c * LANES, (c + 1) * LANES)
        im_t = lambda c: slice(half + c * LANES, half + (c + 1) * LANES)
        xr = [st_s[d, :, re_t(c)] for c in range(n_tiles)]
        xi = [st_s[d, :, im_t(c)] for c in range(n_tiles)]
        for t in order:
            slab = slice(t * B, (t + 1) * B)
            for c in range(n_tiles):
                lr = lam_ref[d, :, re_t(c)]
                li = lam_ref[d, :, im_t(c)]
                nr = lr * xr[c] - li * xi[c] + x_s[c, slab, :]
                ni = lr * xi[c] + li * xr[c] + x_s[n_tiles + c, slab, :]
                xr[c], xi[c] = nr, ni
                x_s[c, slab, :] = nr
                x_s[n_tiles + c, slab, :] = ni
        for c in range(n_tiles):
            st_s[d, :, re_t(c)] = xr[c]
            st_s[d, :, im_t(c)] = xi[c]
        xs = jnp.concatenate([x_s[c].astype(BF16) for c in range(2 * n_tiles)], axis=1)
        y_tm = _dot(xs, cmat_ref[d])
        hi = y_tm.astype(BF16)
        lo = (y_tm - hi.astype(F32)).astype(BF16)
        y_ref[...] = (_dot(perm_ref[1], hi) + _dot(perm_ref[1], lo)).reshape(B, lt, S5_WIDTH)

    direction(0, uf_ref, yf_ref)
    direction(1, ub_ref, yb_ref)


def _s5_scan(u, bmat, lam_b, cmat, B, S):
    lt = min(S5_LT, S)
    n = S // lt
    fwd = lambda j: (0, j, 0)
    bwd = lambda j: (0, n - 1 - j, 0)
    fixed = lambda j: (0, 0, 0)
    r = jnp.arange(B * lt, dtype=jnp.int32)
    to_time_major = jax.nn.one_hot((r % B) * lt + r // B, B * lt, dtype=BF16)
    perm = jnp.stack([to_time_major, to_time_major.T])
    return pl.pallas_call(
        _s5_body,
        grid=(n,),
        in_specs=[pl.BlockSpec((B, lt, S5_WIDTH), fwd), pl.BlockSpec((B, lt, S5_WIDTH), bwd),
                  pl.BlockSpec((2, B * lt, B * lt), fixed),
                  pl.BlockSpec((2, S5_WIDTH, S5_REAL), fixed), pl.BlockSpec((2, B, S5_REAL), fixed),
                  pl.BlockSpec((2, S5_REAL, S5_WIDTH), fixed)],
        out_specs=[pl.BlockSpec((B, lt, S5_WIDTH), fwd), pl.BlockSpec((B, lt, S5_WIDTH), bwd)],
        out_shape=[jax.ShapeDtypeStruct((B, S, S5_WIDTH), F32)] * 2,
        scratch_shapes=[pltpu.VMEM((2, B, S5_REAL), F32), pltpu.VMEM((S5_REAL // LANES, B * lt, LANES), F32)],
        compiler_params=_params(("arbitrary",)), name="s5_scan")(u, u, perm, bmat, lam_b, cmat)


def _s5_operators(a_re, a_im, log_dt, b_re, b_im, c_re, c_im, B):
    dt = jnp.exp(log_dt.astype(F32))[..., None]
    ar, ai = a_re.astype(F32), a_im.astype(F32)
    mag = jnp.exp(ar * dt)
    lr, li = mag * jnp.cos(ai * dt), mag * jnp.sin(ai * dt)
    den = ar * ar + ai * ai
    fr = ((lr - 1.0) * ar + li * ai) / den
    fi = (li * ar - (lr - 1.0) * ai) / den
    br, bi = b_re.astype(F32), b_im.astype(F32)
    bbr = fr[..., None] * br - fi[..., None] * bi
    bbi = fr[..., None] * bi + fi[..., None] * br
    eye = jnp.eye(S5_GROUPS, dtype=F32)
    in_blk = lambda m: jnp.einsum('zgpc,gh->zgchp', m, eye).reshape(2, S5_WIDTH, S5_REAL // 2)
    out_blk = lambda m: jnp.einsum('zgcp,gh->zgphc', m, eye).reshape(2, S5_REAL // 2, S5_WIDTH)
    bmat = jnp.concatenate([in_blk(bbr), in_blk(bbi)], axis=-1).astype(BF16)
    cmat = jnp.concatenate([out_blk(c_re.astype(F32)), -out_blk(c_im.astype(F32))], axis=1).astype(BF16)
    lam = jnp.concatenate([lr.reshape(2, 1, -1), li.reshape(2, 1, -1)], axis=-1)
    return bmat, jnp.broadcast_to(lam, (2, B, S5_REAL)), cmat


def _gelu_tanh(x):
    return 0.5 * x * (1.0 + jnp.tanh(math.sqrt(2.0 / math.pi) * (x + 0.044715 * (x * x * x))))


def _out_proj_body(x_ref, oa_ref, ob_ref, u_ref, yf_ref, yb_ref, dskip_ref, wglu_ref, bglu_ref, wout_ref,
                   g2_ref, wr_ref, br_ref, x1_ref, h2_ref, eid_ref, gate_ref):
    y = dskip_ref[...] * u_ref[...] + yf_ref[...] + yb_ref[...]
    z = _gelu_tanh(y)
    t = _dot(z.astype(BF16), wglu_ref[...]) + bglu_ref[...]
    oc = z * (1.0 / (1.0 + jnp.exp(-t)))
    mix = _dot(oa_ref[...].astype(BF16), wout_ref[0:HG_WIDTH, :])
    mix += _dot(ob_ref[...].astype(BF16), wout_ref[HG_WIDTH:HG_WIDTH + ATTN_WIDTH, :])
    mix += _dot(oc.astype(BF16), wout_ref[HG_WIDTH + ATTN_WIDTH:, :])
    x1 = x_ref[...] + mix
    x1_ref[...] = x1
    h2 = x1 * lax.rsqrt(jnp.mean(x1 * x1, axis=-1, keepdims=True) + NORM_EPS) * g2_ref[...]
    _store_rows(h2_ref, h2)
    h_hi = h2.astype(BF16)
    h_lo = (h2 - h_hi.astype(F32)).astype(BF16)
    logits = _dot(h_hi, wr_ref[0]) + (_dot(h_lo, wr_ref[0]) + _dot(h_hi, wr_ref[1])) + br_ref[...]
    lane = lax.broadcasted_iota(jnp.int32, logits.shape, 1)
    lane_f = lane.astype(F32)
    neg = jnp.float32(-jnp.inf)
    is_g = lane < N_GROUPS
    gl = jnp.where(is_g, logits, neg)
    gmax = jnp.max(gl, axis=-1, keepdims=True)
    gsel = jnp.min(jnp.where(gl == gmax, lane_f, float(ROUTER_LANES)), axis=-1, keepdims=True)
    p_group = 1.0 / jnp.sum(jnp.where(is_g, jnp.exp(logits - gmax), 0.0), axis=-1, keepdims=True)
    first = N_GROUPS + PER_GROUP * gsel
    el = jnp.where((lane_f >= first) & (lane_f < first + PER_GROUP), logits, neg)
    v1 = jnp.max(el, axis=-1, keepdims=True)
    i1 = jnp.min(jnp.where(el == v1, lane_f, float(ROUTER_LANES)), axis=-1, keepdims=True)
    el2 = jnp.where(lane_f == i1, neg, el)
    v2 = jnp.max(el2, axis=-1, keepdims=True)
    i2 = jnp.min(jnp.where(el2 == v2, lane_f, float(ROUTER_LANES)), axis=-1, keepdims=True)
    e2 = jnp.exp(v2 - v1)
    w1 = p_group / (1.0 + e2)
    w2 = p_group * e2 / (1.0 + e2)
    eid_ref[...] = jnp.where(lane == 0, i1 - N_GROUPS, jnp.where(lane == 1, i2 - N_GROUPS, 0.0)).astype(jnp.int32)
    gate_ref[...] = jnp.where(lane == 0, w1, jnp.where(lane == 1, w2, 0.0))


def _out_proj(x, oa, ob, u, yf, yb, dskip, wglu_bf, bglu, wout_bf, g2, wr, br):
    T = x.shape[0]
    tm = min(ROW_TILE, T)
    row = lambda i: (i, 0)
    fixed = lambda i: (0, 0)
    blk = lambda w: pl.BlockSpec((tm, w), row)
    full = lambda a: pl.BlockSpec(a.shape, lambda i: (0,) * a.ndim)
    consts = [dskip, wglu_bf, bglu, wout_bf, g2, wr, br]
    return pl.pallas_call(
        _out_proj_body,
        grid=(T // tm,),
        in_specs=[blk(D_MODEL), blk(HG_WIDTH), blk(ATTN_WIDTH), blk(S5_WIDTH), blk(S5_WIDTH), blk(S5_WIDTH)]
        + [full(a) for a in consts],
        out_specs=[blk(D_MODEL), pl.BlockSpec((tm * ROW_SUB, LANES), row), blk(ROUTER_LANES), blk(ROUTER_LANES)],
        out_shape=[jax.ShapeDtypeStruct((T, D_MODEL), F32), jax.ShapeDtypeStruct((T * ROW_SUB, LANES), F32),
                   jax.ShapeDtypeStruct((T, ROUTER_LANES), jnp.int32),
                   jax.ShapeDtypeStruct((T, ROUTER_LANES), F32)],
        compiler_params=_params(("parallel",)), name="out_proj_router")(x, oa, ob, u, yf, yb, *consts)


def _moe_body(bm, blk_e_ref, n_used_ref, tok_ref, tok_next_ref, dst_ref, h2_hbm, wg_ref, wu_ref, wd_ref,
              out_hbm, xbuf, obuf, wg_s, wu_s, wd_s, gsem, ssem):
    i = pl.program_id(0)
    n_used = n_used_ref[0]
    slot = i % 2

    @pl.when((i == 0) | (blk_e_ref[i] != blk_e_ref[jnp.maximum(i - 1, 0)]))
    def _():
        wg_s[...] = wg_ref[...].astype(BF16)
        wu_s[...] = wu_ref[...].astype(BF16)
        wd_s[...] = wd_ref[...].astype(BF16)

    def gather(idx_ref, s):
        def issue(r, carry):
            pltpu.make_async_copy(h2_hbm.at[idx_ref[0, 0, r]],
                                  xbuf.at[s, pl.ds(pl.multiple_of(r * ROW_SUB, ROW_SUB), ROW_SUB)],
                                  gsem.at[s]).start()
            return carry
        lax.fori_loop(0, bm, issue, 0, unroll=8)

    def scatter_wait(s):
        pltpu.make_async_copy(obuf.at[s], obuf.at[s], ssem.at[s]).wait()

    @pl.when(i == 0)
    def _():
        gather(tok_ref, 0)
        obuf[0] = jnp.zeros(obuf.shape[1:], F32)
        n_assign_rows = out_hbm.shape[0] - 2 * bm

        def clear(r, carry):
            src = obuf.at[0, pl.ds(pl.multiple_of((r % bm) * ROW_SUB, ROW_SUB), ROW_SUB)]
            pltpu.make_async_copy(src, out_hbm.at[n_assign_rows + r], ssem.at[0]).start()
            return carry
        lax.fori_loop(0, 2 * bm, clear, 0, unroll=8)
        scatter_wait(0)
        scatter_wait(0)

    @pl.when(i + 1 < n_used)
    def _():
        gather(tok_next_ref, 1 - slot)

    @pl.when(i < n_used)
    def _():
        pltpu.make_async_copy(xbuf.at[slot], xbuf.at[slot], gsem.at[slot]).wait()
        xb = _load_rows(xbuf.at[slot], bm).astype(BF16)
        g = _dot(xb, wg_s[...])
        u = _dot(xb, wu_s[...])
        _store_rows(obuf.at[slot], _dot((_silu(g) * u).astype(BF16), wd_s[...]))

        def issue(r, carry):
            pltpu.make_async_copy(obuf.at[slot, pl.ds(pl.multiple_of(r * ROW_SUB, ROW_SUB), ROW_SUB)],
                                  out_hbm.at[dst_ref[0, 0, r]], ssem.at[slot]).start()
            return carry
        lax.fori_loop(0, bm, issue, 0, unroll=8)

        @pl.when(i > 0)
        def _():
            scatter_wait(1 - slot)

        @pl.when(i == n_used - 1)
        def _():
            scatter_wait(slot)


def _moe_experts(h2, tables, wg, wu, wd, layer, bm, T):
    row_tok, row_dst, blk_e, n_used = tables
    n_blocks = blk_e.shape[0]
    tok3 = row_tok.reshape(n_blocks, 1, bm)
    dst3 = row_dst.reshape(n_blocks, 1, bm)
    smem_blk = lambda f: pl.BlockSpec((1, 1, bm), f, memory_space=pltpu.SMEM)
    wspec = lambda shape: pl.BlockSpec((None, None) + shape, lambda i, be, nu: (layer, be[i], 0, 0))
    out_rows = TOP_K * T + 2 * bm
    out = pl.pallas_call(
        functools.partial(_moe_body, bm),
        grid_spec=pltpu.PrefetchScalarGridSpec(
            num_scalar_prefetch=2, grid=(n_blocks,),
            in_specs=[smem_blk(lambda i, be, nu: (i, 0, 0)),
                      smem_blk(lambda i, be, nu: (jnp.minimum(i + 1, n_blocks - 1), 0, 0)),
                      smem_blk(lambda i, be, nu: (i, 0, 0)),
                      pl.BlockSpec(memory_space=pl.ANY),
                      wspec((D_MODEL, EXPERT_FF)), wspec((D_MODEL, EXPERT_FF)), wspec((EXPERT_FF, D_MODEL))],
            out_specs=pl.BlockSpec(memory_space=pl.ANY),
            scratch_shapes=[pltpu.VMEM((2, bm * ROW_SUB, LANES), F32), pltpu.VMEM((2, bm * ROW_SUB, LANES), F32),
                            pltpu.VMEM((D_MODEL, EXPERT_FF), BF16), pltpu.VMEM((D_MODEL, EXPERT_FF), BF16),
                            pltpu.VMEM((EXPERT_FF, D_MODEL), BF16),
                            pltpu.SemaphoreType.DMA((2,)), pltpu.SemaphoreType.DMA((2,))]),
        out_shape=jax.ShapeDtypeStruct((out_rows, ROW_SUB, LANES), F32),
        compiler_params=_params(("arbitrary",)), name="moe_experts")(
            blk_e, n_used, tok3, tok3, dst3, h2.reshape(T, ROW_SUB, LANES), wg, wu, wd)
    return out.reshape(out_rows * ROW_SUB, LANES)


def _dispatch_tables(eid, T, bm):
    n_assign = T * TOP_K
    flat_e = eid.reshape(n_assign)
    onehot = (flat_e[:, None] == jnp.arange(N_EXPERTS, dtype=jnp.int32)[None, :]).astype(jnp.int32)
    csum = jnp.cumsum(onehot, axis=0)
    rank = jnp.sum(csum * onehot, axis=1) - 1
    counts = csum[-1]
    padded = ((counts + bm - 1) // bm) * bm
    pends = jnp.cumsum(padded)
    pstarts = pends - padded
    dest = (jnp.sum(onehot * pstarts[None, :], axis=1) + rank).astype(jnp.int32)
    n_blocks = n_assign // bm + N_EXPERTS
    assign = jnp.arange(n_assign, dtype=jnp.int32)
    row_assign = jnp.zeros((n_blocks * bm,), jnp.int32).at[dest].set(assign + 1) - 1
    valid = row_assign >= 0
    row = jnp.arange(n_blocks * bm, dtype=jnp.int32)
    spare = TOP_K * T + ((row // bm) % 2) * bm + row % bm
    row_tok = jnp.where(valid, row_assign // TOP_K, 0)
    row_dst = jnp.where(valid, (row_assign % TOP_K) * T + row_assign // TOP_K, spare)
    blk_start = jnp.arange(n_blocks, dtype=jnp.int32) * bm
    blk_e = jnp.sum((blk_start[:, None] >= pends[None, :]).astype(jnp.int32), axis=1)
    blk_e = jnp.minimum(blk_e, N_EXPERTS - 1)
    n_used = (pends[-1] // bm).astype(jnp.int32).reshape(1)
    return row_tok, row_dst, blk_e, n_used


def _moe(h2, eid, wg_bf, wu_bf, wd_bf, layer):
    T = h2.shape[0] // ROW_SUB
    bm = min(MOE_BM, T)
    return _moe_experts(h2, _dispatch_tables(eid, T, bm), wg_bf, wu_bf, wd_bf, layer, bm, T)


def _final_body(x_ref, y0_ref, y1_ref, gate_ref, g_ref, o_ref):
    gate = gate_ref[...]
    n = x_ref.shape[0]
    x = x_ref[...] + gate[:, 0:1] * _load_rows(y0_ref, n) + gate[:, 1:2] * _load_rows(y1_ref, n)
    o_ref[...] = x * lax.rsqrt(jnp.mean(x * x, axis=-1, keepdims=True) + NORM_EPS) * g_ref[...]


def _final_norm(x, ys, gate, g):
    T = x.shape[0]
    tm = min(ROW_TILE, T)
    nt = T // tm
    row = lambda i: (i, 0)
    return pl.pallas_call(
        _final_body,
        grid=(nt,),
        in_specs=[pl.BlockSpec((tm, D_MODEL), row), pl.BlockSpec((tm * ROW_SUB, LANES), row),
                  pl.BlockSpec((tm * ROW_SUB, LANES), lambda i: (i + nt, 0)), pl.BlockSpec((tm, LANES), row),
                  pl.BlockSpec((1, D_MODEL), lambda i: (0, 0))],
        out_specs=pl.BlockSpec((tm, D_MODEL), row),
        out_shape=jax.ShapeDtypeStruct((T, D_MODEL), F32),
        compiler_params=_params(("parallel",)), name="combine_final_norm")(x, ys, ys, gate, g.reshape(1, D_MODEL))


def kernel(x, norm1_g, w_in, hgrn_lower_bounds, hgrn_norm_g, attn_q_norm_g, attn_k_norm_g, s5_a_re, s5_a_im, s5_log_dt, s5_b_re, s5_b_im, s5_c_re, s5_c_im, s5_d, s5_w_glu, s5_b_glu, w_out, norm2_g, router_group_w, router_group_b, router_expert_w, router_expert_b, expert_w_gate, expert_w_up, expert_w_down, final_norm_g):
    B, S, D = x.shape
    T = B * S
    depth = w_in.shape[0]
    cs = _rope_tables(S)
    lb_all = jnp.cumsum(jax.nn.softmax(hgrn_lower_bounds.astype(F32), axis=0), axis=0)
    lb_all = lb_all - lb_all[0:1]
    wg_bf, wu_bf, wd_bf = expert_w_gate, expert_w_up, expert_w_down
    reps = LANES // HEAD_DIM
    pad = ROUTER_LANES - N_GROUPS - N_EXPERTS

    xt = x.reshape(T, D)
    moe = None
    for l in range(depth):
        if moe is None:
            hg, qkv, u = _in_proj(xt, norm1_g[l], w_in[l].astype(BF16))
        else:
            xt, hg, qkv, u = _in_proj(xt, norm1_g[l], w_in[l].astype(BF16), moe)
        lb = lb_all[l]
        lb_rows = jnp.stack([jnp.log(lb[0]), jnp.log1p(-lb[0]), jnp.log(lb[1]), jnp.log1p(-lb[1])])
        oa = _hgrn(hg, lb_rows, jnp.tile(hgrn_norm_g[l].astype(F32), HG_HEADS).reshape(1, HG_WIDTH), B, S)
        gains = jnp.stack([jnp.tile(attn_q_norm_g[l].astype(F32), reps), jnp.tile(attn_k_norm_g[l].astype(F32), reps)])
        ob = _attention(qkv, cs, gains, B, S)
        bmat, lam_b, cmat = _s5_operators(s5_a_re[l], s5_a_im[l], s5_log_dt[l], s5_b_re[l], s5_b_im[l],
                                          s5_c_re[l], s5_c_im[l], B)
        yf, yb = _s5_scan(u.reshape(B, S, S5_WIDTH), bmat, lam_b, cmat, B, S)
        wr = jnp.concatenate([router_group_w[l].astype(F32), router_expert_w[l].astype(F32),
                              jnp.zeros((D, pad), F32)], axis=1)
        wr_hi = wr.astype(BF16)
        wr = jnp.stack([wr_hi, (wr - wr_hi.astype(F32)).astype(BF16)])
        br =jnp.concatenate([router_group_b[l].astype(F32), router_expert_b[l].astype(F32),
                              jnp.zeros((pad,), F32)]).reshape(1, ROUTER_LANES)
        xt, h2, eid, gate = _out_proj(
            xt, oa, ob, u, yf.reshape(T, S5_WIDTH), yb.reshape(T, S5_WIDTH),
            s5_d[l].astype(F32).reshape(1, S5_WIDTH), s5_w_glu[l].astype(BF16),
            s5_b_glu[l].astype(F32).reshape(1, S5_WIDTH), w_out[l].astype(BF16),
            norm2_g[l].astype(F32).reshape(1, D), wr, br)
        moe = (_moe(h2, eid[:, :TOP_K], wg_bf, wu_bf, wd_bf, l), gate)
    out = _final_norm(xt, moe[0], moe[1], final_norm_g.astype(F32))
    return out.reshape(B, S, D)
```

```python
import functools
import math

import jax
import jax.numpy as jnp
from jax import lax
from jax.experimental import pallas as pl
from jax.experimental.pallas import tpu as pltpu

F32 = jnp.float32
BF16 = jnp.bfloat16
HIGHEST = lax.Precision.HIGHEST

D_MODEL = 1024
HEAD_DIM = 64
HG_WIDTH = 256
S5_WIDTH = 256
ATTN_WIDTH = 512
KV_WIDTH = 128
HG_HEADS = HG_WIDTH // HEAD_DIM
HG_CHUNK = 16
ROPE_THETA = 10000.0
GRID_W = 64
S5_CH = 16
S5_GROUPS = S5_WIDTH // S5_CH
S5_STATE = 64
N_GROUPS = 4
PER_GROUP = 8
N_EXPERTS = N_GROUPS * PER_GROUP
TOP_K = 2
EXPERT_FF = D_MODEL // 2
NORM_EPS = 1e-6
HG_COLS = 5 * HG_WIDTH
QKV_COLS = ATTN_WIDTH + 2 * KV_WIDTH
IN_WIDTH = HG_COLS + QKV_COLS + S5_WIDTH
S5_REAL = 2 * S5_GROUPS * S5_STATE

LANES = 128
HG_BLOCK = 128
ATTN_QB = 256
S5_LT = 32
S5_STAGE = 4
ROW_TILE = 512
MOE_BM = 256
ROUTER_LANES = 128
VMEM_LIMIT = 56 * 1024 * 1024


def _dot(a, b):
    return jnp.dot(a, b, preferred_element_type=F32)


def _dot_hi(a, b):
    return jnp.dot(a, b, preferred_element_type=F32, precision=HIGHEST)


def _dot_nt(a, b):
    return lax.dot_general(a, b, (((1,), (1,)), ((), ())), preferred_element_type=F32)


def _silu(x):
    return x * (1.0 / (1.0 + jnp.exp(-x)))


def _params(sem):
    return pltpu.CompilerParams(dimension_semantics=sem, vmem_limit_bytes=VMEM_LIMIT)


ROW_SUB = D_MODEL // LANES


def _load_rows(ref, n):
    return jnp.concatenate([ref[pl.ds(j, n, stride=ROW_SUB), :] for j in range(ROW_SUB)], axis=1)


def _store_rows(ref, val):
    n = val.shape[0]
    for j in range(ROW_SUB):
        ref[pl.ds(j, n, stride=ROW_SUB), :] = val[:, j * LANES:(j + 1) * LANES]


def _in_proj_body(combine, *refs):
    if combine:
        x_ref, y0_ref, y1_ref, gate_ref, g_ref, w_ref, xo_ref, hg_ref, qkv_ref, u_ref = refs
        gate = gate_ref[...]
        n = x_ref.shape[0]
        x = x_ref[...] + gate[:, 0:1] * _load_rows(y0_ref, n) + gate[:, 1:2] * _load_rows(y1_ref, n)
        xo_ref[...] = x
    else:
        x_ref, g_ref, w_ref, hg_ref, qkv_ref, u_ref = refs
        x = x_ref[...]
    h = x * lax.rsqrt(jnp.mean(x * x, axis=-1, keepdims=True) + NORM_EPS) * g_ref[...]
    hb = h.astype(BF16)
    hg_ref[...] = _dot(hb, w_ref[:, :HG_COLS])
    qkv_ref[...] = _dot(hb, w_ref[:, HG_COLS:HG_COLS + QKV_COLS])
    u_ref[...] = _dot(hb, w_ref[:, HG_COLS + QKV_COLS:])


def _in_proj(x, norm_g, w_bf, moe=None):
    T = x.shape[0]
    tm = min(ROW_TILE, T)
    nt = T // tm
    row = lambda i: (i, 0)
    fixed = lambda i: (0, 0)
    in_specs = [pl.BlockSpec((tm, D_MODEL), row)]
    args = [x]
    out_shape = []
    out_specs = []
    if moe is not None:
        ys, gate = moe
        in_specs += [pl.BlockSpec((tm * ROW_SUB, LANES), row),
                     pl.BlockSpec((tm * ROW_SUB, LANES), lambda i: (i + nt, 0)),
                     pl.BlockSpec((tm, LANES), row)]
        args += [ys, ys, gate]
        out_shape.append(jax.ShapeDtypeStruct((T, D_MODEL), F32))
        out_specs.append(pl.BlockSpec((tm, D_MODEL), row))
    in_specs += [pl.BlockSpec((1, D_MODEL), fixed), pl.BlockSpec((D_MODEL, IN_WIDTH), fixed)]
    args += [norm_g.reshape(1, D_MODEL), w_bf]
    out_shape += [jax.ShapeDtypeStruct((T, HG_COLS), F32), jax.ShapeDtypeStruct((T, QKV_COLS), F32),
                  jax.ShapeDtypeStruct((T, S5_WIDTH), F32)]
    out_specs += [pl.BlockSpec((tm, HG_COLS), row), pl.BlockSpec((tm, QKV_COLS), row),
                  pl.BlockSpec((tm, S5_WIDTH), row)]
    return pl.pallas_call(
        functools.partial(_in_proj_body, moe is not None),
        grid=(nt,), in_specs=in_specs, out_specs=out_specs, out_shape=out_shape,
        compiler_params=_params(("parallel",)), name="in_proj")(*args)


def _log_forget_and_key(z, log_lb, log_1m_lb):
    log_sig = jnp.minimum(z, 0.0) - jnp.log1p(jnp.exp(-jnp.abs(z)))
    a = log_1m_lb + log_sig
    log_f = jnp.maximum(a, log_lb) + jnp.log1p(jnp.exp(-jnp.abs(a - log_lb)))
    return log_f, jnp.exp(a - z)


def _split3(x):
    hi = x.astype(BF16)
    r = x - hi.astype(F32)
    mid = r.astype(BF16)
    lo = (r - mid.astype(F32)).astype(BF16)
    return hi, mid, lo


def _hgrn_body(hg_ref, lb_ref, gn_ref, o_ref, of_s, ob_s, st_s, kv_s):
    S = hg_ref.shape[0]
    n_blk = S // HG_BLOCK
    n_sub = HG_BLOCK // HG_CHUNK
    row = lax.broadcasted_iota(jnp.int32, (HG_BLOCK, HG_BLOCK), 0)
    col = lax.broadcasted_iota(jnp.int32, (HG_BLOCK, HG_BLOCK), 1)
    same = (row // HG_CHUNK) == (col // HG_CHUNK)
    lane_head = lax.broadcasted_iota(jnp.int32, (1, HG_WIDTH), 1) // HEAD_DIM
    col_sub = lax.broadcasted_iota(jnp.int32, (HG_WIDTH, HG_BLOCK), 1) // HG_CHUNK
    same_bf = jnp.where(same, 1.0, 0.0).astype(BF16)

    st_s[...] = jnp.zeros(st_s.shape, F32)

    def block(d, j, out_s):
        r0 = pl.multiple_of(j * HG_BLOCK, HG_BLOCK)
        rows = pl.ds(r0, HG_BLOCK)
        tri = same & ((col <= row) if d == 0 else (col >= row))
        z = hg_ref[rows, (3 + d) * HG_WIDTH:(4 + d) * HG_WIDTH]
        lf, k = _log_forget_and_key(z, lb_ref[2 * d:2 * d + 1, :], lb_ref[2 * d + 1:2 * d + 2, :])
        a_mat = jnp.concatenate([jnp.where(tri, 1.0, 0.0).astype(BF16), same_bf], axis=0)
        hi, mid, lo = _split3(lf)
        acc = _dot(a_mat, hi) + _dot(a_mat, mid) + _dot(a_mat, lo)
        b = acc[:HG_BLOCK]
        tot = acc[HG_BLOCK:]
        q = _silu(hg_ref[rows, 0:HG_WIDTH])
        v = hg_ref[rows, HG_WIDTH:2 * HG_WIDTH]
        q_dec = q * jnp.exp(b)
        k_dec = (k * jnp.exp(-b)).astype(BF16)
        k_end = (k * jnp.exp(tot - b)).astype(BF16)
        dec = jnp.exp(tot)
        q_heads = jnp.concatenate(
            [jnp.where(lane_head == h, q_dec, 0.0).astype(BF16) for h in range(HG_HEADS)], axis=0)
        v_t = v.T
        v_sub = jnp.concatenate(
            [jnp.where(col_sub == c, v_t, 0.0).astype(BF16) for c in range(n_sub)], axis=0)
        kv_s[d] = _dot(v_sub, k_end)
        yield
        scores = _dot_nt(q_heads, k_dec)
        o_blk = jnp.zeros((HG_BLOCK, HG_WIDTH), F32)
        for h in range(HG_HEADS):
            p_h = jnp.where(tri, scores[h * HG_BLOCK:(h + 1) * HG_BLOCK], 0.0).astype(BF16)
            o_blk += _dot(p_h, jnp.where(lane_head == h, v, 0.0).astype(BF16))
        yield
        o_inter = [None] * n_sub
        order = range(n_sub) if d == 0 else range(n_sub - 1, -1, -1)
        for c in order:
            q_c = jnp.concatenate(
                [q_heads[h * HG_BLOCK + c * HG_CHUNK:h * HG_BLOCK + (c + 1) * HG_CHUNK]
                 for h in range(HG_HEADS)], axis=0)
            st = st_s[d]
            o_all = _dot_nt(q_c, st.astype(BF16))
            o_c = jnp.zeros((HG_CHUNK, HG_WIDTH), F32)
            for h in range(HG_HEADS):
                o_c += jnp.where(lane_head == h, o_all[h * HG_CHUNK:(h + 1) * HG_CHUNK], 0.0)
            o_inter[c] = o_c
            st_s[d] = dec[c * HG_CHUNK:c * HG_CHUNK + 1, :] * st + kv_s[d, c * HG_WIDTH:(c + 1) * HG_WIDTH, :]
            yield
        out_s[rows, :] = o_blk + jnp.concatenate(o_inter, axis=0)

    def step(j, carry):
        stages = [block(0, j, of_s), block(1, n_blk - 1 - j, ob_s)]
        while stages:
            stages = [g for g in stages if next(g, True) is None]
        return carry

    lax.fori_loop(0, n_blk, step, 0)

    r2 = lax.broadcasted_iota(jnp.int32, (HG_WIDTH, HG_WIDTH), 0) // HEAD_DIM
    c2 = lax.broadcasted_iota(jnp.int32, (HG_WIDTH, HG_WIDTH), 1) // HEAD_DIM
    head_mean = jnp.where(r2 == c2, 1.0 / HEAD_DIM, 0.0).astype(F32)

    def finish(j, carry):
        rows = pl.ds(pl.multiple_of(j * HG_BLOCK, HG_BLOCK), HG_BLOCK)
        o = of_s[rows, :] + ob_s[rows, :]
        ms = _dot_hi(o * o, head_mean)
        y = o * lax.rsqrt(ms + NORM_EPS) * gn_ref[...]
        o_ref[rows, :] = y * _silu(hg_ref[rows, 2 * HG_WIDTH:3 * HG_WIDTH])
        return carry

    lax.fori_loop(0, n_blk, finish, 0)


def _hgrn(hg, lb_rows, gn, B, S):
    return pl.pallas_call(
        _hgrn_body,
        grid=(B,),
        in_specs=[pl.BlockSpec((S, HG_COLS), lambda b: (b, 0)),
                  pl.BlockSpec((4, HG_WIDTH), lambda b: (0, 0)),
                  pl.BlockSpec((1, HG_WIDTH), lambda b: (0, 0))],
        out_specs=pl.BlockSpec((S, HG_WIDTH), lambda b: (b, 0)),
        out_shape=jax.ShapeDtypeStruct((B * S, HG_WIDTH), F32),
        scratch_shapes=[pltpu.VMEM((S, HG_WIDTH), F32), pltpu.VMEM((S, HG_WIDTH), F32),
                        pltpu.VMEM((2, HG_WIDTH, HG_WIDTH), F32),
                        pltpu.VMEM((2, HG_BLOCK // HG_CHUNK * HG_WIDTH, HG_WIDTH), F32)],
        compiler_params=_params(("parallel",)), name="hgrn2")(hg, lb_rows, gn)


def _attn_body(qkv_ref, cs_ref, gain_ref, o_ref, q_s, k_s, vlo_s, vhi_s):
    S = qkv_ref.shape[0]
    n_qb = S // ATTN_QB
    lane = lax.broadcasted_iota(jnp.int32, (1, LANES), 1)
    lo = lane < HEAD_DIM
    even = (lane % 2) == 0
    r2 = lax.broadcasted_iota(jnp.int32, (LANES, LANES), 0) // HEAD_DIM
    c2 = lax.broadcasted_iota(jnp.int32, (LANES, LANES), 1) // HEAD_DIM
    head_mean = jnp.where(r2 == c2, 1.0 / HEAD_DIM, 0.0).astype(F32)
    n_pairs = ATTN_WIDTH // LANES

    def norm_rope(x, gain, cos, sin):
        ms = _dot_hi(x * x, head_mean)
        y = x * lax.rsqrt(ms + NORM_EPS) * gain
        swapped = jnp.where(even, pltpu.roll(y, LANES - 1, 1), pltpu.roll(y, 1, 1))
        return y * cos + swapped * sin

    def prep(j, carry):
        rows = pl.ds(pl.multiple_of(j * ATTN_QB, ATTN_QB), ATTN_QB)
        cos = cs_ref[0, rows, :]
        sin = cs_ref[1, rows, :]
        for m in range(n_pairs):
            qm = norm_rope(qkv_ref[rows, m * LANES:(m + 1) * LANES], gain_ref[0:1, :], cos, sin)
            qm = qm * (HEAD_DIM ** -0.5)
            q_s[m, 0, rows, :] = jnp.where(lo, qm, 0.0).astype(BF16)
            q_s[m, 1, rows, :] = jnp.where(lo, 0.0, qm).astype(BF16)
        kk = norm_rope(qkv_ref[rows, ATTN_WIDTH:ATTN_WIDTH + LANES], gain_ref[1:2, :], cos, sin)
        kr = pltpu.roll(kk, HEAD_DIM, 1)
        k_s[0, rows, :] = jnp.where(lo, kk, kr).astype(BF16)
        k_s[1, rows, :] = jnp.where(lo, kr, kk).astype(BF16)
        vv = qkv_ref[rows, ATTN_WIDTH + LANES:ATTN_WIDTH + 2 * LANES]
        vr = pltpu.roll(vv, HEAD_DIM, 1)
        ones_hi = jnp.where(lane == HEAD_DIM, 1.0, 0.0)
        ones_lo = jnp.where(lane == 0, 1.0, 0.0)
        vlo_s[0, rows, :] = jnp.where(lo, vv, ones_hi).astype(BF16)
        vhi_s[0, rows, :] = jnp.where(lo, ones_lo, vr).astype(BF16)
        vlo_s[1, rows, :] = jnp.where(lo, vr, ones_hi).astype(BF16)
        vhi_s[1, rows, :] = jnp.where(lo, ones_lo, vv).astype(BF16)
        return carry

    lax.fori_loop(0, n_qb, prep, 0)

    def attend(j, carry):
        rows = pl.ds(pl.multiple_of(j * ATTN_QB, ATTN_QB), ATTN_QB)
        for m in range(n_pairs):
            kv = m // (n_pairs // 2)
            q2 = jnp.concatenate([q_s[m, 0, rows, :], q_s[m, 1, rows, :]], axis=0)
            s = _dot_nt(q2, k_s[kv])
            p = jnp.exp((s - jnp.max(s, axis=-1, keepdims=True)).astype(BF16))
            o_lo = _dot(p[:ATTN_QB], vlo_s[kv])
            o_hi = _dot(p[ATTN_QB:], vhi_s[kv])
            o_ref[rows, m * LANES:(m + 1) * LANES] = jnp.where(
                lo, o_lo / o_lo[:, HEAD_DIM:HEAD_DIM + 1], o_hi / o_hi[:, 0:1])
        return carry

    lax.fori_loop(0, n_qb, attend, 0)


def _attention(qkv, cs, gains, B, S):
    n_pairs = ATTN_WIDTH // LANES
    return pl.pallas_call(
        _attn_body,
        grid=(B,),
        in_specs=[pl.BlockSpec((S, QKV_COLS), lambda b: (b, 0)),
                  pl.BlockSpec((2, S, LANES), lambda b: (0, 0, 0)),
                  pl.BlockSpec((2, LANES), lambda b: (0, 0))],
        out_specs=pl.BlockSpec((S, ATTN_WIDTH), lambda b: (b, 0)),
        out_shape=jax.ShapeDtypeStruct((B * S, ATTN_WIDTH), F32),
        scratch_shapes=[pltpu.VMEM((n_pairs, 2, S, LANES), BF16), pltpu.VMEM((2, S, LANES), BF16),
                        pltpu.VMEM((2, S, LANES), BF16), pltpu.VMEM((2, S, LANES), BF16)],
        compiler_params=_params(("parallel",)), name="gqa_attention")(qkv, cs, gains)


def _rope_tables(S):
    rows_count = S // GRID_W
    rows = jnp.repeat(jnp.arange(rows_count, dtype=F32), GRID_W)
    cols = jnp.tile(jnp.arange(GRID_W, dtype=F32), rows_count)
    pairs = HEAD_DIM // 4
    freqs = jnp.power(jnp.float32(ROPE_THETA), -jnp.arange(pairs, dtype=F32) / pairs)
    ang = jnp.concatenate([rows[:, None] * freqs, cols[:, None] * freqs], axis=-1)
    cos = jnp.repeat(jnp.cos(ang), 2, axis=-1)
    sin = jnp.repeat(jnp.sin(ang), 2, axis=-1) * jnp.tile(jnp.array([-1.0, 1.0], F32), HEAD_DIM // 2)
    reps = LANES // HEAD_DIM
    return jnp.stack([jnp.tile(cos, (1, reps)), jnp.tile(sin, (1, reps))])


def _s5_body(uf_ref, ub_ref, perm_ref, bmat_ref, lam_ref, cmat_ref, yf_ref, yb_ref, st_s, x_s):
    B, lt, _ = uf_ref.shape
    half = S5_REAL // 2

    @pl.when(pl.program_id(0) == 0)
    def _():
        st_s[...] = jnp.zeros(st_s.shape, F32)

    n_tiles = half // LANES

    def direction(d, u_ref, y_ref):
        u = u_ref[...].reshape(B * lt, S5_WIDTH).astype(BF16)
        u_tm = _dot(perm_ref[0], u).astype(BF16)
        bu = _dot(u_tm, bmat_ref[d])
        for c in range(2 * n_tiles):
            x_s[d, c] = bu[:, c * LANES:(c + 1) * LANES]
        yield
        order = range(lt) if d == 0 else range(lt - 1, -1, -1)
        re_t = lambda c: slice(c * LANES, (c + 1) * LANES)
        im_t = lambda c: slice(half + c * LANES, half + (c + 1) * LANES)
        xr = [st_s[d, :, re_t(c)] for c in range(n_tiles)]
        xi = [st_s[d, :, im_t(c)] for c in range(n_tiles)]
        for n, t in enumerate(order):
            slab = slice(t * B, (t + 1) * B)
            for c in range(n_tiles):
                lr = lam_ref[d, :, re_t(c)]
                li = lam_ref[d, :, im_t(c)]
                nr = lr * xr[c] - li * xi[c] + x_s[d, c, slab, :]
                ni = lr * xi[c] + li * xr[c] + x_s[d, n_tiles + c, slab, :]
                xr[c], xi[c] = nr, ni
                x_s[d, c, slab, :] = nr
                x_s[d, n_tiles + c, slab, :] = ni
            if n % S5_STAGE == S5_STAGE - 1:
                yield
        for c in range(n_tiles):
            st_s[d, :, re_t(c)] = xr[c]
            st_s[d, :, im_t(c)] = xi[c]
        xs = jnp.concatenate([x_s[d, c].astype(BF16) for c in range(2 * n_tiles)], axis=1)
        y_tm = _dot(xs, cmat_ref[d])
        hi = y_tm.astype(BF16)
        lo = (y_tm - hi.astype(F32)).astype(BF16)
        y_ref[...] = (_dot(perm_ref[1], hi) + _dot(perm_ref[1], lo)).reshape(B, lt, S5_WIDTH)

    stages = [direction(0, uf_ref, yf_ref), direction(1, ub_ref, yb_ref)]
    while stages:
        stages = [g for g in stages if next(g, True) is None]


def _s5_scan(u, bmat, lam_b, cmat, B, S):
    lt = min(S5_LT, S)
    n = S // lt
    fwd = lambda j: (0, j, 0)
    bwd = lambda j: (0, n - 1 - j, 0)
    fixed = lambda j: (0, 0, 0)
    r = jnp.arange(B * lt, dtype=jnp.int32)
    to_time_major = jax.nn.one_hot((r % B) * lt + r // B, B * lt, dtype=BF16)
    perm = jnp.stack([to_time_major, to_time_major.T])
    return pl.pallas_call(
        _s5_body,
        grid=(n,),
        in_specs=[pl.BlockSpec((B, lt, S5_WIDTH), fwd), pl.BlockSpec((B, lt, S5_WIDTH), bwd),
                  pl.BlockSpec((2, B * lt, B * lt), fixed),
                  pl.BlockSpec((2, S5_WIDTH, S5_REAL), fixed), pl.BlockSpec((2, B, S5_REAL), fixed),
                  pl.BlockSpec((2, S5_REAL, S5_WIDTH), fixed)],
        out_specs=[pl.BlockSpec((B, lt, S5_WIDTH), fwd), pl.BlockSpec((B, lt, S5_WIDTH), bwd)],
        out_shape=[jax.ShapeDtypeStruct((B, S, S5_WIDTH), F32)] * 2,
        scratch_shapes=[pltpu.VMEM((2, B, S5_REAL), F32), pltpu.VMEM((2, S5_REAL // LANES, B * lt, LANES), F32)],
        compiler_params=_params(("arbitrary",)), name="s5_scan")(u, u, perm, bmat, lam_b, cmat)


def _s5_operators(a_re, a_im, log_dt, b_re, b_im, c_re, c_im, B):
    dt = jnp.exp(log_dt.astype(F32))[..., None]
    ar, ai = a_re.astype(F32), a_im.astype(F32)
    mag = jnp.exp(ar * dt)
    lr, li = mag * jnp.cos(ai * dt), mag * jnp.sin(ai * dt)
    den = ar * ar + ai * ai
    fr = ((lr - 1.0) * ar + li * ai) / den
    fi = (li * ar - (lr - 1.0) * ai) / den
    br, bi = b_re.astype(F32), b_im.astype(F32)
    bbr = fr[..., None] * br - fi[..., None] * bi
    bbi = fr[..., None] * bi + fi[..., None] * br
    eye = jnp.eye(S5_GROUPS, dtype=F32)
    in_blk = lambda m: jnp.einsum('zgpc,gh->zgchp', m, eye).reshape(2, S5_WIDTH, S5_REAL // 2)
    out_blk = lambda m: jnp.einsum('zgcp,gh->zgphc', m, eye).reshape(2, S5_REAL // 2, S5_WIDTH)
    bmat = jnp.concatenate([in_blk(bbr), in_blk(bbi)], axis=-1).astype(BF16)
    cmat = jnp.concatenate([out_blk(c_re.astype(F32)), -out_blk(c_im.astype(F32))], axis=1).astype(BF16)
    lam = jnp.concatenate([lr.reshape(2, 1, -1), li.reshape(2, 1, -1)], axis=-1)
    return bmat, jnp.broadcast_to(lam, (2, B, S5_REAL)), cmat


def _gelu_tanh(x):
    return 0.5 * x * (1.0 + jnp.tanh(math.sqrt(2.0 / math.pi) * (x + 0.044715 * (x * x * x))))


def _out_proj_body(x_ref, oa_ref, ob_ref, u_ref, yf_ref, yb_ref, dskip_ref, wglu_ref, bglu_ref, wout_ref,
                   g2_ref, wr_ref, br_ref, x1_ref, h2_ref, eid_ref, gate_ref):
    y = dskip_ref[...] * u_ref[...] + yf_ref[...] + yb_ref[...]
    z = _gelu_tanh(y)
    t = _dot(z.astype(BF16), wglu_ref[...]) + bglu_ref[...]
    oc = z * (1.0 / (1.0 + jnp.exp(-t)))
    mix = _dot(oa_ref[...].astype(BF16), wout_ref[0:HG_WIDTH, :])
    mix += _dot(ob_ref[...].astype(BF16), wout_ref[HG_WIDTH:HG_WIDTH + ATTN_WIDTH, :])
    mix += _dot(oc.astype(BF16), wout_ref[HG_WIDTH + ATTN_WIDTH:, :])
    x1 = x_ref[...] + mix
    x1_ref[...] = x1
    h2 = x1 * lax.rsqrt(jnp.mean(x1 * x1, axis=-1, keepdims=True) + NORM_EPS) * g2_ref[...]
    _store_rows(h2_ref, h2)
    h_hi = h2.astype(BF16)
    h_lo = (h2 - h_hi.astype(F32)).astype(BF16)
    logits = _dot(h_hi, wr_ref[0]) + (_dot(h_lo, wr_ref[0]) + _dot(h_hi, wr_ref[1])) + br_ref[...]
    lane = lax.broadcasted_iota(jnp.int32, logits.shape, 1)
    lane_f = lane.astype(F32)
    neg = jnp.float32(-jnp.inf)
    is_g = lane < N_GROUPS
    gl = jnp.where(is_g, logits, neg)
    gmax = jnp.max(gl, axis=-1, keepdims=True)
    gsel = jnp.min(jnp.where(gl == gmax, lane_f, float(ROUTER_LANES)), axis=-1, keepdims=True)
    p_group = 1.0 / jnp.sum(jnp.where(is_g, jnp.exp(logits - gmax), 0.0), axis=-1, keepdims=True)
    first = N_GROUPS + PER_GROUP * gsel
    el = jnp.where((lane_f >= first) & (lane_f < first + PER_GROUP), logits, neg)
    v1 = jnp.max(el, axis=-1, keepdims=True)
    i1 = jnp.min(jnp.where(el == v1, lane_f, float(ROUTER_LANES)), axis=-1, keepdims=True)
    el2 = jnp.where(lane_f == i1, neg, el)
    v2 = jnp.max(el2, axis=-1, keepdims=True)
    i2 = jnp.min(jnp.where(el2 == v2, lane_f, float(ROUTER_LANES)), axis=-1, keepdims=True)
    e2 = jnp.exp(v2 - v1)
    w1 = p_group / (1.0 + e2)
    w2 = p_group * e2 / (1.0 + e2)
    eid_ref[...] = jnp.where(lane == 0, i1 - N_GROUPS, jnp.where(lane == 1, i2 - N_GROUPS, 0.0)).astype(jnp.int32)
    gate_ref[...] = jnp.where(lane == 0, w1, jnp.where(lane == 1, w2, 0.0))


def _out_proj(x, oa, ob, u, yf, yb, dskip, wglu_bf, bglu, wout_bf, g2, wr, br):
    T = x.shape[0]
    tm = min(ROW_TILE, T)
    row = lambda i: (i, 0)
    fixed = lambda i: (0, 0)
    blk = lambda w: pl.BlockSpec((tm, w), row)
    full = lambda a: pl.BlockSpec(a.shape, lambda i: (0,) * a.ndim)
    consts = [dskip, wglu_bf, bglu, wout_bf, g2, wr, br]
    return pl.pallas_call(
        _out_proj_body,
        grid=(T // tm,),
        in_specs=[blk(D_MODEL), blk(HG_WIDTH), blk(ATTN_WIDTH), blk(S5_WIDTH), blk(S5_WIDTH), blk(S5_WIDTH)]
        + [full(a) for a in consts],
        out_specs=[blk(D_MODEL), pl.BlockSpec((tm * ROW_SUB, LANES), row), blk(ROUTER_LANES), blk(ROUTER_LANES)],
        out_shape=[jax.ShapeDtypeStruct((T, D_MODEL), F32), jax.ShapeDtypeStruct((T * ROW_SUB, LANES), F32),
                   jax.ShapeDtypeStruct((T, ROUTER_LANES), jnp.int32),
                   jax.ShapeDtypeStruct((T, ROUTER_LANES), F32)],
        compiler_params=_params(("parallel",)), name="out_proj_router")(x, oa, ob, u, yf, yb, *consts)


def _moe_body(bm, blk_e_ref, n_used_ref, tok_ref, tok_next_ref, dst_ref, h2_hbm, wg_ref, wu_ref, wd_ref,
              out_hbm, xbuf, obuf, wg_s, wu_s, wd_s, gsem, ssem):
    i = pl.program_id(0)
    n_used = n_used_ref[0]
    slot = i % 2

    @pl.when((i == 0) | (blk_e_ref[i] != blk_e_ref[jnp.maximum(i - 1, 0)]))
    def _():
        wg_s[...] = wg_ref[...].astype(BF16)
        wu_s[...] = wu_ref[...].astype(BF16)
        wd_s[...] = wd_ref[...].astype(BF16)

    def gather(idx_ref, s):
        def issue(r, carry):
            pltpu.make_async_copy(h2_hbm.at[idx_ref[0, 0, r]],
                                  xbuf.at[s, pl.ds(pl.multiple_of(r * ROW_SUB, ROW_SUB), ROW_SUB)],
                                  gsem.at[s]).start()
            return carry
        lax.fori_loop(0, bm, issue, 0, unroll=8)

    def scatter_wait(s):
        pltpu.make_async_copy(obuf.at[s], obuf.at[s], ssem.at[s]).wait()

    @pl.when(i == 0)
    def _():
        gather(tok_ref, 0)
        obuf[0] = jnp.zeros(obuf.shape[1:], F32)
        n_assign_rows = out_hbm.shape[0] - 2 * bm

        def clear(r, carry):
            src = obuf.at[0, pl.ds(pl.multiple_of((r % bm) * ROW_SUB, ROW_SUB), ROW_SUB)]
            pltpu.make_async_copy(src, out_hbm.at[n_assign_rows + r], ssem.at[0]).start()
            return carry
        lax.fori_loop(0, 2 * bm, clear, 0, unroll=8)
        scatter_wait(0)
        scatter_wait(0)

    @pl.when(i + 1 < n_used)
    def _():
        gather(tok_next_ref, 1 - slot)

    @pl.when(i < n_used)
    def _():
        pltpu.make_async_copy(xbuf.at[slot], xbuf.at[slot], gsem.at[slot]).wait()
        xb = _load_rows(xbuf.at[slot], bm).astype(BF16)
        g = _dot(xb, wg_s[...])
        u = _dot(xb, wu_s[...])
        _store_rows(obuf.at[slot], _dot((_silu(g) * u).astype(BF16), wd_s[...]))

        def issue(r, carry):
            pltpu.make_async_copy(obuf.at[slot, pl.ds(pl.multiple_of(r * ROW_SUB, ROW_SUB), ROW_SUB)],
                                  out_hbm.at[dst_ref[0, 0, r]], ssem.at[slot]).start()
            return carry
        lax.fori_loop(0, bm, issue, 0, unroll=8)

        @pl.when(i > 0)
        def _():
            scatter_wait(1 - slot)

        @pl.when(i == n_used - 1)
        def _():
            scatter_wait(slot)


def _moe_experts(h2, tables, wg, wu, wd, layer, bm, T):
    row_tok, row_dst, blk_e, n_used = tables
    n_blocks = blk_e.shape[0]
    tok3 = row_tok.reshape(n_blocks, 1, bm)
    dst3 = row_dst.reshape(n_blocks, 1, bm)
    smem_blk = lambda f: pl.BlockSpec((1, 1, bm), f, memory_space=pltpu.SMEM)
    wspec = lambda shape: pl.BlockSpec((None, None) + shape, lambda i, be, nu: (layer, be[i], 0, 0))
    out_rows = TOP_K * T + 2 * bm
    out = pl.pallas_call(
        functools.partial(_moe_body, bm),
        grid_spec=pltpu.PrefetchScalarGridSpec(
            num_scalar_prefetch=2, grid=(n_blocks,),
            in_specs=[smem_blk(lambda i, be, nu: (i, 0, 0)),
                      smem_blk(lambda i, be, nu: (jnp.minimum(i + 1, n_blocks - 1), 0, 0)),
                      smem_blk(lambda i, be, nu: (i, 0, 0)),
                      pl.BlockSpec(memory_space=pl.ANY),
                      wspec((D_MODEL, EXPERT_FF)), wspec((D_MODEL, EXPERT_FF)), wspec((EXPERT_FF, D_MODEL))],
            out_specs=pl.BlockSpec(memory_space=pl.ANY),
            scratch_shapes=[pltpu.VMEM((2, bm * ROW_SUB, LANES), F32), pltpu.VMEM((2, bm * ROW_SUB, LANES), F32),
                            pltpu.VMEM((D_MODEL, EXPERT_FF), BF16), pltpu.VMEM((D_MODEL, EXPERT_FF), BF16),
                            pltpu.VMEM((EXPERT_FF, D_MODEL), BF16),
                            pltpu.SemaphoreType.DMA((2,)), pltpu.SemaphoreType.DMA((2,))]),
        out_shape=jax.ShapeDtypeStruct((out_rows, ROW_SUB, LANES), F32),
        compiler_params=_params(("arbitrary",)), name="moe_experts")(
            blk_e, n_used, tok3, tok3, dst3, h2.reshape(T, ROW_SUB, LANES), wg, wu, wd)
    return out.reshape(out_rows * ROW_SUB, LANES)


def _dispatch_tables(eid, T, bm):
    n_assign = T * TOP_K
    flat_e = eid.reshape(n_assign)
    onehot = (flat_e[:, None] == jnp.arange(N_EXPERTS, dtype=jnp.int32)[None, :]).astype(jnp.int32)
    csum = jnp.cumsum(onehot, axis=0)
    rank = jnp.sum(csum * onehot, axis=1) - 1
    counts = csum[-1]
    padded = ((counts + bm - 1) // bm) * bm
    pends = jnp.cumsum(padded)
    pstarts = pends - padded
    dest = (jnp.sum(onehot * pstarts[None, :], axis=1) + rank).astype(jnp.int32)
    n_blocks = n_assign // bm + N_EXPERTS
    assign = jnp.arange(n_assign, dtype=jnp.int32)
    row_assign = jnp.zeros((n_blocks * bm,), jnp.int32).at[dest].set(assign + 1) - 1
    valid = row_assign >= 0
    row = jnp.arange(n_blocks * bm, dtype=jnp.int32)
    spare = TOP_K * T + ((row // bm) % 2) * bm + row % bm
    row_tok = jnp.where(valid, row_assign // TOP_K, 0)
    row_dst = jnp.where(valid, (row_assign % TOP_K) * T + row_assign // TOP_K, spare)
    blk_start = jnp.arange(n_blocks, dtype=jnp.int32) * bm
    blk_e = jnp.sum((blk_start[:, None] >= pends[None, :]).astype(jnp.int32), axis=1)
    blk_e = jnp.minimum(blk_e, N_EXPERTS - 1)
    n_used = (pends[-1] // bm).astype(jnp.int32).reshape(1)
    return row_tok, row_dst, blk_e, n_used


def _moe(h2, eid, wg_bf, wu_bf, wd_bf, layer):
    T = h2.shape[0] // ROW_SUB
    bm = min(MOE_BM, T)
    return _moe_experts(h2, _dispatch_tables(eid, T, bm), wg_bf, wu_bf, wd_bf, layer, bm, T)


def _final_body(x_ref, y0_ref, y1_ref, gate_ref, g_ref, o_ref):
    gate = gate_ref[...]
    n = x_ref.shape[0]
    x = x_ref[...] + gate[:, 0:1] * _load_rows(y0_ref, n) + gate[:, 1:2] * _load_rows(y1_ref, n)
    o_ref[...] = x * lax.rsqrt(jnp.mean(x * x, axis=-1, keepdims=True) + NORM_EPS) * g_ref[...]


def _final_norm(x, ys, gate, g):
    T = x.shape[0]
    tm = min(ROW_TILE, T)
    nt = T // tm
    row = lambda i: (i, 0)
    return pl.pallas_call(
        _final_body,
        grid=(nt,),
        in_specs=[pl.BlockSpec((tm, D_MODEL), row), pl.BlockSpec((tm * ROW_SUB, LANES), row),
                  pl.BlockSpec((tm * ROW_SUB, LANES), lambda i: (i + nt, 0)), pl.BlockSpec((tm, LANES), row),
                  pl.BlockSpec((1, D_MODEL), lambda i: (0, 0))],
        out_specs=pl.BlockSpec((tm, D_MODEL), row),
        out_shape=jax.ShapeDtypeStruct((T, D_MODEL), F32),
        compiler_params=_params(("parallel",)), name="combine_final_norm")(x, ys, ys, gate, g.reshape(1, D_MODEL))


def kernel(x, norm1_g, w_in, hgrn_lower_bounds, hgrn_norm_g, attn_q_norm_g, attn_k_norm_g, s5_a_re, s5_a_im, s5_log_dt, s5_b_re, s5_b_im, s5_c_re, s5_c_im, s5_d, s5_w_glu, s5_b_glu, w_out, norm2_g, router_group_w, router_group_b, router_expert_w, router_expert_b, expert_w_gate, expert_w_up, expert_w_down, final_norm_g):
    B, S, D = x.shape
    T = B * S
    depth = w_in.shape[0]
    cs = _rope_tables(S)
    lb_all = jnp.cumsum(jax.nn.softmax(hgrn_lower_bounds.astype(F32), axis=0), axis=0)
    lb_all = lb_all - lb_all[0:1]
    wg_bf, wu_bf, wd_bf = expert_w_gate, expert_w_up, expert_w_down
    reps = LANES // HEAD_DIM
    pad = ROUTER_LANES - N_GROUPS - N_EXPERTS

    xt = x.reshape(T, D)
    moe = None
    for l in range(depth):
        if moe is None:
            hg, qkv, u = _in_proj(xt, norm1_g[l], w_in[l].astype(BF16))
        else:
            xt, hg, qkv, u = _in_proj(xt, norm1_g[l], w_in[l].astype(BF16), moe)
        lb = lb_all[l]
        lb_rows = jnp.stack([jnp.log(lb[0]), jnp.log1p(-lb[0]), jnp.log(lb[1]), jnp.log1p(-lb[1])])
        oa = _hgrn(hg, lb_rows, jnp.tile(hgrn_norm_g[l].astype(F32), HG_HEADS).reshape(1, HG_WIDTH), B, S)
        gains = jnp.stack([jnp.tile(attn_q_norm_g[l].astype(F32), reps), jnp.tile(attn_k_norm_g[l].astype(F32), reps)])
        ob = _attention(qkv, cs, gains, B, S)
        bmat, lam_b, cmat = _s5_operators(s5_a_re[l], s5_a_im[l], s5_log_dt[l], s5_b_re[l], s5_b_im[l],
                                          s5_c_re[l], s5_c_im[l], B)
        yf, yb = _s5_scan(u.reshape(B, S, S5_WIDTH), bmat, lam_b, cmat, B, S)
        wr = jnp.concatenate([router_group_w[l].astype(F32), router_expert_w[l].astype(F32),
                              jnp.zeros((D, pad), F32)], axis=1)
        wr_hi = wr.astype(BF16)
        wr = jnp.stack([wr_hi, (wr - wr_hi.astype(F32)).astype(BF16)])
        br =jnp.concatenate([router_group_b[l].astype(F32), router_expert_b[l].astype(F32),
                              jnp.zeros((pad,), F32)]).reshape(1, ROUTER_LANES)
        xt, h2, eid, gate = _out_proj(
            xt, oa, ob, u, yf.reshape(T, S5_WIDTH), yb.reshape(T, S5_WIDTH),
            s5_d[l].astype(F32).reshape(1, S5_WIDTH), s5_w_glu[l].astype(BF16),
            s5_b_glu[l].astype(F32).reshape(1, S5_WIDTH), w_out[l].astype(BF16),
            norm2_g[l].astype(F32).reshape(1, D), wr, br)
        moe = (_moe(h2, eid[:, :TOP_K], wg_bf, wu_bf, wd_bf, l), gate)
    out = _final_norm(xt, moe[0], moe[1], final_norm_g.astype(F32))
    return out.reshape(B, S, D)
```

```python
import functools
import math

import jax
import jax.numpy as jnp
from jax import lax
from jax.experimental import pallas as pl
from jax.experimental.pallas import tpu as pltpu

F32 = jnp.float32
BF16 = jnp.bfloat16
HIGHEST = lax.Precision.HIGHEST

D_MODEL = 1024
HEAD_DIM = 64
HG_WIDTH = 256
S5_WIDTH = 256
ATTN_WIDTH = 512
KV_WIDTH = 128
HG_HEADS = HG_WIDTH // HEAD_DIM
HG_CHUNK = 16
ROPE_THETA = 10000.0
GRID_W = 64
S5_CH = 16
S5_GROUPS = S5_WIDTH // S5_CH
S5_STATE = 64
N_GROUPS = 4
PER_GROUP = 8
N_EXPERTS = N_GROUPS * PER_GROUP
TOP_K = 2
EXPERT_FF = D_MODEL // 2
NORM_EPS = 1e-6
HG_COLS = 5 * HG_WIDTH
QKV_COLS = ATTN_WIDTH + 2 * KV_WIDTH
IN_WIDTH = HG_COLS + QKV_COLS + S5_WIDTH
S5_REAL = 2 * S5_GROUPS * S5_STATE

LANES = 128
HG_BLOCK = 128
ATTN_QB = 512
S5_LT = 32
S5_STAGE = 4
ROW_TILE = 512
MOE_BM = 256
ROUTER_LANES = 128
VMEM_LIMIT = 56 * 1024 * 1024


def _dot(a, b):
    return jnp.dot(a, b, preferred_element_type=F32)


def _dot_hi(a, b):
    return jnp.dot(a, b, preferred_element_type=F32, precision=HIGHEST)


def _dot_nt(a, b):
    return lax.dot_general(a, b, (((1,), (1,)), ((), ())), preferred_element_type=F32)


def _silu(x):
    return x * (1.0 / (1.0 + jnp.exp(-x)))


def _params(sem):
    return pltpu.CompilerParams(dimension_semantics=sem, vmem_limit_bytes=VMEM_LIMIT)


ROW_SUB = D_MODEL // LANES


def _load_rows(ref, n):
    return jnp.concatenate([ref[pl.ds(j, n, stride=ROW_SUB), :] for j in range(ROW_SUB)], axis=1)


def _store_rows(ref, val):
    n = val.shape[0]
    for j in range(ROW_SUB):
        ref[pl.ds(j, n, stride=ROW_SUB), :] = val[:, j * LANES:(j + 1) * LANES]


def _in_proj_body(combine, *refs):
    if combine:
        x_ref, y0_ref, y1_ref, gate_ref, g_ref, w_ref, xo_ref, hg_ref, qkv_ref, u_ref = refs
        gate = gate_ref[...]
        n = x_ref.shape[0]
        x = x_ref[...] + gate[:, 0:1] * _load_rows(y0_ref, n) + gate[:, 1:2] * _load_rows(y1_ref, n)
        xo_ref[...] = x
    else:
        x_ref, g_ref, w_ref, hg_ref, qkv_ref, u_ref = refs
        x = x_ref[...]
    h = x * lax.rsqrt(jnp.mean(x * x, axis=-1, keepdims=True) + NORM_EPS) * g_ref[...]
    hb = h.astype(BF16)
    hg_ref[...] = _dot(hb, w_ref[:, :HG_COLS])
    qkv_ref[...] = _dot(hb, w_ref[:, HG_COLS:HG_COLS + QKV_COLS])
    u_ref[...] = _dot(hb, w_ref[:, HG_COLS + QKV_COLS:])


def _in_proj(x, norm_g, w_bf, moe=None):
    T = x.shape[0]
    tm = min(ROW_TILE, T)
    nt = T // tm
    row = lambda i: (i, 0)
    fixed = lambda i: (0, 0)
    in_specs = [pl.BlockSpec((tm, D_MODEL), row)]
    args = [x]
    out_shape = []
    out_specs = []
    if moe is not None:
        ys, gate = moe
        in_specs += [pl.BlockSpec((tm * ROW_SUB, LANES), row),
                     pl.BlockSpec((tm * ROW_SUB, LANES), lambda i: (i + nt, 0)),
                     pl.BlockSpec((tm, LANES), row)]
        args += [ys, ys, gate]
        out_shape.append(jax.ShapeDtypeStruct((T, D_MODEL), F32))
        out_specs.append(pl.BlockSpec((tm, D_MODEL), row))
    in_specs += [pl.BlockSpec((1, D_MODEL), fixed), pl.BlockSpec((D_MODEL, IN_WIDTH), fixed)]
    args += [norm_g.reshape(1, D_MODEL), w_bf]
    out_shape += [jax.ShapeDtypeStruct((T, HG_COLS), F32), jax.ShapeDtypeStruct((T, QKV_COLS), F32),
                  jax.ShapeDtypeStruct((T, S5_WIDTH), F32)]
    out_specs += [pl.BlockSpec((tm, HG_COLS), row), pl.BlockSpec((tm, QKV_COLS), row),
                  pl.BlockSpec((tm, S5_WIDTH), row)]
    return pl.pallas_call(
        functools.partial(_in_proj_body, moe is not None),
        grid=(nt,), in_specs=in_specs, out_specs=out_specs, out_shape=out_shape,
        compiler_params=_params(("parallel",)), name="in_proj")(*args)


def _log_forget_and_key(z, log_lb, log_1m_lb):
    log_sig = jnp.minimum(z, 0.0) - jnp.log1p(jnp.exp(-jnp.abs(z)))
    a = log_1m_lb + log_sig
    log_f = jnp.maximum(a, log_lb) + jnp.log1p(jnp.exp(-jnp.abs(a - log_lb)))
    return log_f, jnp.exp(a - z)


def _split3(x):
    hi = x.astype(BF16)
    r = x - hi.astype(F32)
    mid = r.astype(BF16)
    lo = (r - mid.astype(F32)).astype(BF16)
    return hi, mid, lo


def _hgrn_body(hg_ref, lb_ref, gn_ref, o_ref, of_s, ob_s, st_s, kv_s):
    S = hg_ref.shape[0]
    n_blk = S // HG_BLOCK
    n_sub = HG_BLOCK // HG_CHUNK
    row = lax.broadcasted_iota(jnp.int32, (HG_BLOCK, HG_BLOCK), 0)
    col = lax.broadcasted_iota(jnp.int32, (HG_BLOCK, HG_BLOCK), 1)
    same = (row // HG_CHUNK) == (col // HG_CHUNK)
    lane_head = lax.broadcasted_iota(jnp.int32, (1, HG_WIDTH), 1) // HEAD_DIM
    col_sub = lax.broadcasted_iota(jnp.int32, (HG_WIDTH, HG_BLOCK), 1) // HG_CHUNK
    same_bf = jnp.where(same, 1.0, 0.0).astype(BF16)

    st_s[...] = jnp.zeros(st_s.shape, F32)

    def block(d, j, out_s):
        r0 = pl.multiple_of(j * HG_BLOCK, HG_BLOCK)
        rows = pl.ds(r0, HG_BLOCK)
        tri = same & ((col <= row) if d == 0 else (col >= row))
        z = hg_ref[rows, (3 + d) * HG_WIDTH:(4 + d) * HG_WIDTH]
        lf, k = _log_forget_and_key(z, lb_ref[2 * d:2 * d + 1, :], lb_ref[2 * d + 1:2 * d + 2, :])
        a_mat = jnp.concatenate([jnp.where(tri, 1.0, 0.0).astype(BF16), same_bf], axis=0)
        hi, mid, lo = _split3(lf)
        acc = _dot(a_mat, hi) + _dot(a_mat, mid) + _dot(a_mat, lo)
        b = acc[:HG_BLOCK]
        tot = acc[HG_BLOCK:]
        q = _silu(hg_ref[rows, 0:HG_WIDTH])
        v = hg_ref[rows, HG_WIDTH:2 * HG_WIDTH]
        q_dec = q * jnp.exp(b)
        k_dec = (k * jnp.exp(-b)).astype(BF16)
        k_end = (k * jnp.exp(tot - b)).astype(BF16)
        dec = jnp.exp(tot)
        q_heads = jnp.concatenate(
            [jnp.where(lane_head == h, q_dec, 0.0).astype(BF16) for h in range(HG_HEADS)], axis=0)
        v_t = v.T
        v_sub = jnp.concatenate(
            [jnp.where(col_sub == c, v_t, 0.0).astype(BF16) for c in range(n_sub)], axis=0)
        kv_s[d] = _dot(v_sub, k_end)
        yield
        scores = _dot_nt(q_heads, k_dec)
        o_blk = jnp.zeros((HG_BLOCK, HG_WIDTH), F32)
        for h in range(HG_HEADS):
            p_h = jnp.where(tri, scores[h * HG_BLOCK:(h + 1) * HG_BLOCK], 0.0).astype(BF16)
            o_blk += _dot(p_h, jnp.where(lane_head == h, v, 0.0).astype(BF16))
        yield
        o_inter = [None] * n_sub
        order = range(n_sub) if d == 0 else range(n_sub - 1, -1, -1)
        for c in order:
            q_c = jnp.concatenate(
                [q_heads[h * HG_BLOCK + c * HG_CHUNK:h * HG_BLOCK + (c + 1) * HG_CHUNK]
                 for h in range(HG_HEADS)], axis=0)
            st = st_s[d]
            o_all = _dot_nt(q_c, st.astype(BF16))
            o_c = jnp.zeros((HG_CHUNK, HG_WIDTH), F32)
            for h in range(HG_HEADS):
                o_c += jnp.where(lane_head == h, o_all[h * HG_CHUNK:(h + 1) * HG_CHUNK], 0.0)
            o_inter[c] = o_c
            st_s[d] = dec[c * HG_CHUNK:c * HG_CHUNK + 1, :] * st + kv_s[d, c * HG_WIDTH:(c + 1) * HG_WIDTH, :]
            yield
        out_s[rows, :] = o_blk + jnp.concatenate(o_inter, axis=0)

    def step(j, carry):
        stages = [block(0, j, of_s), block(1, n_blk - 1 - j, ob_s)]
        while stages:
            stages = [g for g in stages if next(g, True) is None]
        return carry

    lax.fori_loop(0, n_blk, step, 0)

    r2 = lax.broadcasted_iota(jnp.int32, (HG_WIDTH, HG_WIDTH), 0) // HEAD_DIM
    c2 = lax.broadcasted_iota(jnp.int32, (HG_WIDTH, HG_WIDTH), 1) // HEAD_DIM
    head_mean = jnp.where(r2 == c2, 1.0 / HEAD_DIM, 0.0).astype(F32)

    def finish(j, carry):
        rows = pl.ds(pl.multiple_of(j * HG_BLOCK, HG_BLOCK), HG_BLOCK)
        o = of_s[rows, :] + ob_s[rows, :]
        ms = _dot_hi(o * o, head_mean)
        y = o * lax.rsqrt(ms + NORM_EPS) * gn_ref[...]
        o_ref[rows, :] = y * _silu(hg_ref[rows, 2 * HG_WIDTH:3 * HG_WIDTH])
        return carry

    lax.fori_loop(0, n_blk, finish, 0)


def _hgrn(hg, lb_rows, gn, B, S):
    return pl.pallas_call(
        _hgrn_body,
        grid=(B,),
        in_specs=[pl.BlockSpec((S, HG_COLS), lambda b: (b, 0)),
                  pl.BlockSpec((4, HG_WIDTH), lambda b: (0, 0)),
                  pl.BlockSpec((1, HG_WIDTH), lambda b: (0, 0))],
        out_specs=pl.BlockSpec((S, HG_WIDTH), lambda b: (b, 0)),
        out_shape=jax.ShapeDtypeStruct((B * S, HG_WIDTH), F32),
        scratch_shapes=[pltpu.VMEM((S, HG_WIDTH), F32), pltpu.VMEM((S, HG_WIDTH), F32),
                        pltpu.VMEM((2, HG_WIDTH, HG_WIDTH), F32),
                        pltpu.VMEM((2, HG_BLOCK // HG_CHUNK * HG_WIDTH, HG_WIDTH), F32)],
        compiler_params=_params(("parallel",)), name="hgrn2")(hg, lb_rows, gn)


def _attn_body(qkv_ref, cs_ref, gain_ref, o_ref, q_s, k_s, vlo_s, vhi_s):
    S = qkv_ref.shape[0]
    n_qb = S // ATTN_QB
    lane = lax.broadcasted_iota(jnp.int32, (1, LANES), 1)
    lo = lane < HEAD_DIM
    even = (lane % 2) == 0
    r2 = lax.broadcasted_iota(jnp.int32, (LANES, LANES), 0) // HEAD_DIM
    c2 = lax.broadcasted_iota(jnp.int32, (LANES, LANES), 1) // HEAD_DIM
    head_mean = jnp.where(r2 == c2, 1.0 / HEAD_DIM, 0.0).astype(F32)
    n_pairs = ATTN_WIDTH // LANES

    def norm_rope(x, gain, cos, sin):
        ms = _dot_hi(x * x, head_mean)
        y = x * lax.rsqrt(ms + NORM_EPS) * gain
        swapped = jnp.where(even, pltpu.roll(y, LANES - 1, 1), pltpu.roll(y, 1, 1))
        return y * cos + swapped * sin

    def prep(j, carry):
        rows = pl.ds(pl.multiple_of(j * ATTN_QB, ATTN_QB), ATTN_QB)
        cos = cs_ref[0, rows, :]
        sin = cs_ref[1, rows, :]
        for m in range(n_pairs):
            qm = norm_rope(qkv_ref[rows, m * LANES:(m + 1) * LANES], gain_ref[0:1, :], cos, sin)
            qm = qm * (HEAD_DIM ** -0.5)
            q_s[m, 0, rows, :] = jnp.where(lo, qm, 0.0).astype(BF16)
            q_s[m, 1, rows, :] = jnp.where(lo, 0.0, qm).astype(BF16)
        kk = norm_rope(qkv_ref[rows, ATTN_WIDTH:ATTN_WIDTH + LANES], gain_ref[1:2, :], cos, sin)
        kr = pltpu.roll(kk, HEAD_DIM, 1)
        k_s[0, rows, :] = jnp.where(lo, kk, kr).astype(BF16)
        k_s[1, rows, :] = jnp.where(lo, kr, kk).astype(BF16)
        vv = qkv_ref[rows, ATTN_WIDTH + LANES:ATTN_WIDTH + 2 * LANES]
        vr = pltpu.roll(vv, HEAD_DIM, 1)
        ones_hi = jnp.where(lane == HEAD_DIM, 1.0, 0.0)
        ones_lo = jnp.where(lane == 0, 1.0, 0.0)
        vlo_s[0, rows, :] = jnp.where(lo, vv, ones_hi).astype(BF16)
        vhi_s[0, rows, :] = jnp.where(lo, ones_lo, vr).astype(BF16)
        vlo_s[1, rows, :] = jnp.where(lo, vr, ones_hi).astype(BF16)
        vhi_s[1, rows, :] = jnp.where(lo, ones_lo, vv).astype(BF16)
        return carry

    lax.fori_loop(0, n_qb, prep, 0)

    def attend(j, carry):
        rows = pl.ds(pl.multiple_of(j * ATTN_QB, ATTN_QB), ATTN_QB)
        for m in range(n_pairs):
            kv = m // (n_pairs // 2)
            q2 = jnp.concatenate([q_s[m, 0, rows, :], q_s[m, 1, rows, :]], axis=0)
            s = _dot_nt(q2, k_s[kv])
            p = jnp.exp((s - jnp.max(s, axis=-1, keepdims=True)).astype(BF16))
            o_lo = _dot(p[:ATTN_QB], vlo_s[kv])
            o_hi = _dot(p[ATTN_QB:], vhi_s[kv])
            o_ref[rows, m * LANES:(m + 1) * LANES] = jnp.where(
                lo, o_lo / o_lo[:, HEAD_DIM:HEAD_DIM + 1], o_hi / o_hi[:, 0:1])
        return carry

    lax.fori_loop(0, n_qb, attend, 0)


def _attention(qkv, cs, gains, B, S):
    n_pairs = ATTN_WIDTH // LANES
    return pl.pallas_call(
        _attn_body,
        grid=(B,),
        in_specs=[pl.BlockSpec((S, QKV_COLS), lambda b: (b, 0)),
                  pl.BlockSpec((2, S, LANES), lambda b: (0, 0, 0)),
                  pl.BlockSpec((2, LANES), lambda b: (0, 0))],
        out_specs=pl.BlockSpec((S, ATTN_WIDTH), lambda b: (b, 0)),
        out_shape=jax.ShapeDtypeStruct((B * S, ATTN_WIDTH), F32),
        scratch_shapes=[pltpu.VMEM((n_pairs, 2, S, LANES), BF16), pltpu.VMEM((2, S, LANES), BF16),
                        pltpu.VMEM((2, S, LANES), BF16), pltpu.VMEM((2, S, LANES), BF16)],
        compiler_params=_params(("parallel",)), name="gqa_attention")(qkv, cs, gains)


def _rope_tables(S):
    rows_count = S // GRID_W
    rows = jnp.repeat(jnp.arange(rows_count, dtype=F32), GRID_W)
    cols = jnp.tile(jnp.arange(GRID_W, dtype=F32), rows_count)
    pairs = HEAD_DIM // 4
    freqs = jnp.power(jnp.float32(ROPE_THETA), -jnp.arange(pairs, dtype=F32) / pairs)
    ang = jnp.concatenate([rows[:, None] * freqs, cols[:, None] * freqs], axis=-1)
    cos = jnp.repeat(jnp.cos(ang), 2, axis=-1)
    sin = jnp.repeat(jnp.sin(ang), 2, axis=-1) * jnp.tile(jnp.array([-1.0, 1.0], F32), HEAD_DIM // 2)
    reps = LANES // HEAD_DIM
    return jnp.stack([jnp.tile(cos, (1, reps)), jnp.tile(sin, (1, reps))])


def _s5_body(uf_ref, ub_ref, perm_ref, bmat_ref, lam_ref, cmat_ref, yf_ref, yb_ref, st_s, x_s):
    B, lt, _ = uf_ref.shape
    half = S5_REAL // 2

    @pl.when(pl.program_id(0) == 0)
    def _():
        st_s[...] = jnp.zeros(st_s.shape, F32)

    n_tiles = half // LANES

    def direction(d, u_ref, y_ref):
        u = u_ref[...].reshape(B * lt, S5_WIDTH).astype(BF16)
        u_tm = _dot(perm_ref[0], u).astype(BF16)
        bu = _dot(u_tm, bmat_ref[d])
        for c in range(2 * n_tiles):
            x_s[d, c] = bu[:, c * LANES:(c + 1) * LANES]
        yield
        order = range(lt) if d == 0 else range(lt - 1, -1, -1)
        re_t = lambda c: slice(c * LANES, (c + 1) * LANES)
        im_t = lambda c: slice(half + c * LANES, half + (c + 1) * LANES)
        xr = [st_s[d, :, re_t(c)] for c in range(n_tiles)]
        xi = [st_s[d, :, im_t(c)] for c in range(n_tiles)]
        for n, t in enumerate(order):
            slab = slice(t * B, (t + 1) * B)
            for c in range(n_tiles):
                lr = lam_ref[d, :, re_t(c)]
                li = lam_ref[d, :, im_t(c)]
                nr = lr * xr[c] - li * xi[c] + x_s[d, c, slab, :]
                ni = lr * xi[c] + li * xr[c] + x_s[d, n_tiles + c, slab, :]
                xr[c], xi[c] = nr, ni
                x_s[d, c, slab, :] = nr
                x_s[d, n_tiles + c, slab, :] = ni
            if n % S5_STAGE == S5_STAGE - 1:
                yield
        for c in range(n_tiles):
            st_s[d, :, re_t(c)] = xr[c]
            st_s[d, :, im_t(c)] = xi[c]
        xs = jnp.concatenate([x_s[d, c].astype(BF16) for c in range(2 * n_tiles)], axis=1)
        y_tm = _dot(xs, cmat_ref[d])
        hi = y_tm.astype(BF16)
        lo = (y_tm - hi.astype(F32)).astype(BF16)
        y_ref[...] = (_dot(perm_ref[1], hi) + _dot(perm_ref[1], lo)).reshape(B, lt, S5_WIDTH)

    stages = [direction(0, uf_ref, yf_ref), direction(1, ub_ref, yb_ref)]
    while stages:
        stages = [g for g in stages if next(g, True) is None]


def _s5_scan(u, bmat, lam_b, cmat, B, S):
    lt = min(S5_LT, S)
    n = S // lt
    fwd = lambda j: (0, j, 0)
    bwd = lambda j: (0, n - 1 - j, 0)
    fixed = lambda j: (0, 0, 0)
    r = jnp.arange(B * lt, dtype=jnp.int32)
    to_time_major = jax.nn.one_hot((r % B) * lt + r // B, B * lt, dtype=BF16)
    perm = jnp.stack([to_time_major, to_time_major.T])
    return pl.pallas_call(
        _s5_body,
        grid=(n,),
        in_specs=[pl.BlockSpec((B, lt, S5_WIDTH), fwd), pl.BlockSpec((B, lt, S5_WIDTH), bwd),
                  pl.BlockSpec((2, B * lt, B * lt), fixed),
                  pl.BlockSpec((2, S5_WIDTH, S5_REAL), fixed), pl.BlockSpec((2, B, S5_REAL), fixed),
                  pl.BlockSpec((2, S5_REAL, S5_WIDTH), fixed)],
        out_specs=[pl.BlockSpec((B, lt, S5_WIDTH), fwd), pl.BlockSpec((B, lt, S5_WIDTH), bwd)],
        out_shape=[jax.ShapeDtypeStruct((B, S, S5_WIDTH), F32)] * 2,
        scratch_shapes=[pltpu.VMEM((2, B, S5_REAL), F32), pltpu.VMEM((2, S5_REAL // LANES, B * lt, LANES), F32)],
        compiler_params=_params(("arbitrary",)), name="s5_scan")(u, u, perm, bmat, lam_b, cmat)


def _s5_operators(a_re, a_im, log_dt, b_re, b_im, c_re, c_im, B):
    dt = jnp.exp(log_dt.astype(F32))[..., None]
    ar, ai = a_re.astype(F32), a_im.astype(F32)
    mag = jnp.exp(ar * dt)
    lr, li = mag * jnp.cos(ai * dt), mag * jnp.sin(ai * dt)
    den = ar * ar + ai * ai
    fr = ((lr - 1.0) * ar + li * ai) / den
    fi = (li * ar - (lr - 1.0) * ai) / den
    br, bi = b_re.astype(F32), b_im.astype(F32)
    bbr = fr[..., None] * br - fi[..., None] * bi
    bbi = fr[..., None] * bi + fi[..., None] * br
    eye = jnp.eye(S5_GROUPS, dtype=F32)
    in_blk = lambda m: jnp.einsum('zgpc,gh->zgchp', m, eye).reshape(2, S5_WIDTH, S5_REAL // 2)
    out_blk = lambda m: jnp.einsum('zgcp,gh->zgphc', m, eye).reshape(2, S5_REAL // 2, S5_WIDTH)
    bmat = jnp.concatenate([in_blk(bbr), in_blk(bbi)], axis=-1).astype(BF16)
    cmat = jnp.concatenate([out_blk(c_re.astype(F32)), -out_blk(c_im.astype(F32))], axis=1).astype(BF16)
    lam = jnp.concatenate([lr.reshape(2, 1, -1), li.reshape(2, 1, -1)], axis=-1)
    return bmat, jnp.broadcast_to(lam, (2, B, S5_REAL)), cmat


def _gelu_tanh(x):
    return 0.5 * x * (1.0 + jnp.tanh(math.sqrt(2.0 / math.pi) * (x + 0.044715 * (x * x * x))))


def _out_proj_body(x_ref, oa_ref, ob_ref, u_ref, yf_ref, yb_ref, dskip_ref, wglu_ref, bglu_ref, wout_ref,
                   g2_ref, wr_ref, br_ref, x1_ref, h2_ref, eid_ref, gate_ref):
    y = dskip_ref[...] * u_ref[...] + yf_ref[...] + yb_ref[...]
    z = _gelu_tanh(y)
    t = _dot(z.astype(BF16), wglu_ref[...]) + bglu_ref[...]
    oc = z * (1.0 / (1.0 + jnp.exp(-t)))
    mix = _dot(oa_ref[...].astype(BF16), wout_ref[0:HG_WIDTH, :])
    mix += _dot(ob_ref[...].astype(BF16), wout_ref[HG_WIDTH:HG_WIDTH + ATTN_WIDTH, :])
    mix += _dot(oc.astype(BF16), wout_ref[HG_WIDTH + ATTN_WIDTH:, :])
    x1 = x_ref[...] + mix
    x1_ref[...] = x1
    h2 = x1 * lax.rsqrt(jnp.mean(x1 * x1, axis=-1, keepdims=True) + NORM_EPS) * g2_ref[...]
    _store_rows(h2_ref, h2)
    h_hi = h2.astype(BF16)
    h_lo = (h2 - h_hi.astype(F32)).astype(BF16)
    logits = _dot(h_hi, wr_ref[0]) + (_dot(h_lo, wr_ref[0]) + _dot(h_hi, wr_ref[1])) + br_ref[...]
    lane = lax.broadcasted_iota(jnp.int32, logits.shape, 1)
    lane_f = lane.astype(F32)
    neg = jnp.float32(-jnp.inf)
    is_g = lane < N_GROUPS
    gl = jnp.where(is_g, logits, neg)
    gmax = jnp.max(gl, axis=-1, keepdims=True)
    gsel = jnp.min(jnp.where(gl == gmax, lane_f, float(ROUTER_LANES)), axis=-1, keepdims=True)
    p_group = 1.0 / jnp.sum(jnp.where(is_g, jnp.exp(logits - gmax), 0.0), axis=-1, keepdims=True)
    first = N_GROUPS + PER_GROUP * gsel
    el = jnp.where((lane_f >= first) & (lane_f < first + PER_GROUP), logits, neg)
    v1 = jnp.max(el, axis=-1, keepdims=True)
    i1 = jnp.min(jnp.where(el == v1, lane_f, float(ROUTER_LANES)), axis=-1, keepdims=True)
    el2 = jnp.where(lane_f == i1, neg, el)
    v2 = jnp.max(el2, axis=-1, keepdims=True)
    i2 = jnp.min(jnp.where(el2 == v2, lane_f, float(ROUTER_LANES)), axis=-1, keepdims=True)
    e2 = jnp.exp(v2 - v1)
    w1 = p_group / (1.0 + e2)
    w2 = p_group * e2 / (1.0 + e2)
    eid_ref[...] = jnp.where(lane == 0, i1 - N_GROUPS, jnp.where(lane == 1, i2 - N_GROUPS, 0.0)).astype(jnp.int32)
    gate_ref[...] = jnp.where(lane == 0, w1, jnp.where(lane == 1, w2, 0.0))


def _out_proj(x, oa, ob, u, yf, yb, dskip, wglu_bf, bglu, wout_bf, g2, wr, br):
    T = x.shape[0]
    tm = min(ROW_TILE, T)
    row = lambda i: (i, 0)
    fixed = lambda i: (0, 0)
    blk = lambda w: pl.BlockSpec((tm, w), row)
    full = lambda a: pl.BlockSpec(a.shape, lambda i: (0,) * a.ndim)
    consts = [dskip, wglu_bf, bglu, wout_bf, g2, wr, br]
    return pl.pallas_call(
        _out_proj_body,
        grid=(T // tm,),
        in_specs=[blk(D_MODEL), blk(HG_WIDTH), blk(ATTN_WIDTH), blk(S5_WIDTH), blk(S5_WIDTH), blk(S5_WIDTH)]
        + [full(a) for a in consts],
        out_specs=[blk(D_MODEL), pl.BlockSpec((tm * ROW_SUB, LANES), row), blk(ROUTER_LANES), blk(ROUTER_LANES)],
        out_shape=[jax.ShapeDtypeStruct((T, D_MODEL), F32), jax.ShapeDtypeStruct((T * ROW_SUB, LANES), F32),
                   jax.ShapeDtypeStruct((T, ROUTER_LANES), jnp.int32),
                   jax.ShapeDtypeStruct((T, ROUTER_LANES), F32)],
        compiler_params=_params(("parallel",)), name="out_proj_router")(x, oa, ob, u, yf, yb, *consts)


def _moe_body(bm, blk_e_ref, n_used_ref, tok_ref, tok_next_ref, dst_ref, h2_hbm, wg_ref, wu_ref, wd_ref,
              out_hbm, xbuf, obuf, wg_s, wu_s, wd_s, gsem, ssem):
    i = pl.program_id(0)
    n_used = n_used_ref[0]
    slot = i % 2

    @pl.when((i == 0) | (blk_e_ref[i] != blk_e_ref[jnp.maximum(i - 1, 0)]))
    def _():
        wg_s[...] = wg_ref[...].astype(BF16)
        wu_s[...] = wu_ref[...].astype(BF16)
        wd_s[...] = wd_ref[...].astype(BF16)

    def gather(idx_ref, s):
        def issue(r, carry):
            pltpu.make_async_copy(h2_hbm.at[idx_ref[0, 0, r]],
                                  xbuf.at[s, pl.ds(pl.multiple_of(r * ROW_SUB, ROW_SUB), ROW_SUB)],
                                  gsem.at[s]).start()
            return carry
        lax.fori_loop(0, bm, issue, 0, unroll=8)

    def scatter_wait(s):
        pltpu.make_async_copy(obuf.at[s], obuf.at[s], ssem.at[s]).wait()

    @pl.when(i == 0)
    def _():
        gather(tok_ref, 0)
        obuf[0] = jnp.zeros(obuf.shape[1:], F32)
        n_assign_rows = out_hbm.shape[0] - 2 * bm

        def clear(r, carry):
            src = obuf.at[0, pl.ds(pl.multiple_of((r % bm) * ROW_SUB, ROW_SUB), ROW_SUB)]
            pltpu.make_async_copy(src, out_hbm.at[n_assign_rows + r], ssem.at[0]).start()
            return carry
        lax.fori_loop(0, 2 * bm, clear, 0, unroll=8)
        scatter_wait(0)
        scatter_wait(0)

    @pl.when(i + 1 < n_used)
    def _():
        gather(tok_next_ref, 1 - slot)

    @pl.when(i < n_used)
    def _():
        pltpu.make_async_copy(xbuf.at[slot], xbuf.at[slot], gsem.at[slot]).wait()
        xb = _load_rows(xbuf.at[slot], bm).astype(BF16)
        g = _dot(xb, wg_s[...])
        u = _dot(xb, wu_s[...])
        _store_rows(obuf.at[slot], _dot((_silu(g) * u).astype(BF16), wd_s[...]))

        def issue(r, carry):
            pltpu.make_async_copy(obuf.at[slot, pl.ds(pl.multiple_of(r * ROW_SUB, ROW_SUB), ROW_SUB)],
                                  out_hbm.at[dst_ref[0, 0, r]], ssem.at[slot]).start()
            return carry
        lax.fori_loop(0, bm, issue, 0, unroll=8)

        @pl.when(i > 0)
        def _():
            scatter_wait(1 - slot)

        @pl.when(i == n_used - 1)
        def _():
            scatter_wait(slot)


def _moe_experts(h2, tables, wg, wu, wd, layer, bm, T):
    row_tok, row_dst, blk_e, n_used = tables
    n_blocks = blk_e.shape[0]
    tok3 = row_tok.reshape(n_blocks, 1, bm)
    dst3 = row_dst.reshape(n_blocks, 1, bm)
    smem_blk = lambda f: pl.BlockSpec((1, 1, bm), f, memory_space=pltpu.SMEM)
    wspec = lambda shape: pl.BlockSpec((None, None) + shape, lambda i, be, nu: (layer, be[i], 0, 0))
    out_rows = TOP_K * T + 2 * bm
    out = pl.pallas_call(
        functools.partial(_moe_body, bm),
        grid_spec=pltpu.PrefetchScalarGridSpec(
            num_scalar_prefetch=2, grid=(n_blocks,),
            in_specs=[smem_blk(lambda i, be, nu: (i, 0, 0)),
                      smem_blk(lambda i, be, nu: (jnp.minimum(i + 1, n_blocks - 1), 0, 0)),
                      smem_blk(lambda i, be, nu: (i, 0, 0)),
                      pl.BlockSpec(memory_space=pl.ANY),
                      wspec((D_MODEL, EXPERT_FF)), wspec((D_MODEL, EXPERT_FF)), wspec((EXPERT_FF, D_MODEL))],
            out_specs=pl.BlockSpec(memory_space=pl.ANY),
            scratch_shapes=[pltpu.VMEM((2, bm * ROW_SUB, LANES), F32), pltpu.VMEM((2, bm * ROW_SUB, LANES), F32),
                            pltpu.VMEM((D_MODEL, EXPERT_FF), BF16), pltpu.VMEM((D_MODEL, EXPERT_FF), BF16),
                            pltpu.VMEM((EXPERT_FF, D_MODEL), BF16),
                            pltpu.SemaphoreType.DMA((2,)), pltpu.SemaphoreType.DMA((2,))]),
        out_shape=jax.ShapeDtypeStruct((out_rows, ROW_SUB, LANES), F32),
        compiler_params=_params(("arbitrary",)), name="moe_experts")(
            blk_e, n_used, tok3, tok3, dst3, h2.reshape(T, ROW_SUB, LANES), wg, wu, wd)
    return out.reshape(out_rows * ROW_SUB, LANES)


def _dispatch_tables(eid, T, bm):
    n_assign = T * TOP_K
    flat_e = eid.reshape(n_assign)
    onehot = (flat_e[:, None] == jnp.arange(N_EXPERTS, dtype=jnp.int32)[None, :]).astype(jnp.int32)
    csum = jnp.cumsum(onehot, axis=0)
    rank = jnp.sum(csum * onehot, axis=1) - 1
    counts = csum[-1]
    padded = ((counts + bm - 1) // bm) * bm
    pends = jnp.cumsum(padded)
    pstarts = pends - padded
    dest = (jnp.sum(onehot * pstarts[None, :], axis=1) + rank).astype(jnp.int32)
    n_blocks = n_assign // bm + N_EXPERTS
    assign = jnp.arange(n_assign, dtype=jnp.int32)
    row_assign = jnp.zeros((n_blocks * bm,), jnp.int32).at[dest].set(
        assign + 1, unique_indices=True, mode='promise_in_bounds') - 1
    valid = row_assign >= 0
    row = jnp.arange(n_blocks * bm, dtype=jnp.int32)
    spare = TOP_K * T + ((row // bm) % 2) * bm + row % bm
    row_tok = jnp.where(valid, row_assign // TOP_K, 0)
    row_dst = jnp.where(valid, (row_assign % TOP_K) * T + row_assign // TOP_K, spare)
    blk_start = jnp.arange(n_blocks, dtype=jnp.int32) * bm
    blk_e = jnp.sum((blk_start[:, None] >= pends[None, :]).astype(jnp.int32), axis=1)
    blk_e = jnp.minimum(blk_e, N_EXPERTS - 1)
    n_used = (pends[-1] // bm).astype(jnp.int32).reshape(1)
    return row_tok, row_dst, blk_e, n_used


def _moe(h2, eid, wg_bf, wu_bf, wd_bf, layer):
    T = h2.shape[0] // ROW_SUB
    bm = min(MOE_BM, T)
    return _moe_experts(h2, _dispatch_tables(eid, T, bm), wg_bf, wu_bf, wd_bf, layer, bm, T)


def _final_body(x_ref, y0_ref, y1_ref, gate_ref, g_ref, o_ref):
    gate = gate_ref[...]
    n = x_ref.shape[0]
    x = x_ref[...] + gate[:, 0:1] * _load_rows(y0_ref, n) + gate[:, 1:2] * _load_rows(y1_ref, n)
    o_ref[...] = x * lax.rsqrt(jnp.mean(x * x, axis=-1, keepdims=True) + NORM_EPS) * g_ref[...]


def _final_norm(x, ys, gate, g):
    T = x.shape[0]
    tm = min(ROW_TILE, T)
    nt = T // tm
    row = lambda i: (i, 0)
    return pl.pallas_call(
        _final_body,
        grid=(nt,),
        in_specs=[pl.BlockSpec((tm, D_MODEL), row), pl.BlockSpec((tm * ROW_SUB, LANES), row),
                  pl.BlockSpec((tm * ROW_SUB, LANES), lambda i: (i + nt, 0)), pl.BlockSpec((tm, LANES), row),
                  pl.BlockSpec((1, D_MODEL), lambda i: (0, 0))],
        out_specs=pl.BlockSpec((tm, D_MODEL), row),
        out_shape=jax.ShapeDtypeStruct((T, D_MODEL), F32),
        compiler_params=_params(("parallel",)), name="combine_final_norm")(x, ys, ys, gate, g.reshape(1, D_MODEL))


def kernel(x, norm1_g, w_in, hgrn_lower_bounds, hgrn_norm_g, attn_q_norm_g, attn_k_norm_g, s5_a_re, s5_a_im, s5_log_dt, s5_b_re, s5_b_im, s5_c_re, s5_c_im, s5_d, s5_w_glu, s5_b_glu, w_out, norm2_g, router_group_w, router_group_b, router_expert_w, router_expert_b, expert_w_gate, expert_w_up, expert_w_down, final_norm_g):
    B, S, D = x.shape
    T = B * S
    depth = w_in.shape[0]
    cs = _rope_tables(S)
    lb_all = jnp.cumsum(jax.nn.softmax(hgrn_lower_bounds.astype(F32), axis=0), axis=0)
    lb_all = lb_all - lb_all[0:1]
    wg_bf, wu_bf, wd_bf = expert_w_gate, expert_w_up, expert_w_down
    reps = LANES // HEAD_DIM
    pad = ROUTER_LANES - N_GROUPS - N_EXPERTS

    xt = x.reshape(T, D)
    moe = None
    for l in range(depth):
        if moe is None:
            hg, qkv, u = _in_proj(xt, norm1_g[l], w_in[l].astype(BF16))
        else:
            xt, hg, qkv, u = _in_proj(xt, norm1_g[l], w_in[l].astype(BF16), moe)
        lb = lb_all[l]
        lb_rows = jnp.stack([jnp.log(lb[0]), jnp.log1p(-lb[0]), jnp.log(lb[1]), jnp.log1p(-lb[1])])
        oa = _hgrn(hg, lb_rows, jnp.tile(hgrn_norm_g[l].astype(F32), HG_HEADS).reshape(1, HG_WIDTH), B, S)
        gains = jnp.stack([jnp.tile(attn_q_norm_g[l].astype(F32), reps), jnp.tile(attn_k_norm_g[l].astype(F32), reps)])
        ob = _attention(qkv, cs, gains, B, S)
        bmat, lam_b, cmat = _s5_operators(s5_a_re[l], s5_a_im[l], s5_log_dt[l], s5_b_re[l], s5_b_im[l],
                                          s5_c_re[l], s5_c_im[l], B)
        yf, yb = _s5_scan(u.reshape(B, S, S5_WIDTH), bmat, lam_b, cmat, B, S)
        wr = jnp.concatenate([router_group_w[l].astype(F32), router_expert_w[l].astype(F32),
                              jnp.zeros((D, pad), F32)], axis=1)
        wr_hi = wr.astype(BF16)
        wr = jnp.stack([wr_hi, (wr - wr_hi.astype(F32)).astype(BF16)])
        br =jnp.concatenate([router_group_b[l].astype(F32), router_expert_b[l].astype(F32),
                              jnp.zeros((pad,), F32)]).reshape(1, ROUTER_LANES)
        xt, h2, eid, gate = _out_proj(
            xt, oa, ob, u, yf.reshape(T, S5_WIDTH), yb.reshape(T, S5_WIDTH),
            s5_d[l].astype(F32).reshape(1, S5_WIDTH), s5_w_glu[l].astype(BF16),
            s5_b_glu[l].astype(F32).reshape(1, S5_WIDTH), w_out[l].astype(BF16),
            norm2_g[l].astype(F32).reshape(1, D), wr, br)
        moe = (_moe(h2, eid[:, :TOP_K], wg_bf, wu_bf, wd_bf, l), gate)
    out = _final_norm(xt, moe[0], moe[1], final_norm_g.astype(F32))
    return out.reshape(B, S, D)
```

```python
import functools
import math

import jax
import jax.numpy as jnp
from jax import lax
from jax.experimental import pallas as pl
from jax.experimental.pallas import tpu as pltpu

F32 = jnp.float32
BF16 = jnp.bfloat16

D_MODEL = 1024
HEAD_DIM = 64
HG_WIDTH = 256
S5_WIDTH = 256
ATTN_WIDTH = 512
KV_WIDTH = 128
HG_HEADS = HG_WIDTH // HEAD_DIM
HG_CHUNK = 16
ROPE_THETA = 10000.0
GRID_W = 64
S5_CH = 16
S5_GROUPS = S5_WIDTH // S5_CH
S5_STATE = 64
N_GROUPS = 4
PER_GROUP = 8
N_EXPERTS = N_GROUPS * PER_GROUP
TOP_K = 2
EXPERT_FF = D_MODEL // 2
NORM_EPS = 1e-6
HG_COLS = 5 * HG_WIDTH
QKV_COLS = ATTN_WIDTH + 2 * KV_WIDTH
IN_WIDTH = HG_COLS + QKV_COLS + S5_WIDTH
S5_REAL = 2 * S5_GROUPS * S5_STATE

LANES = 128
HG_BLOCK = 128
ATTN_QB = 512
S5_LT = 32
S5_STAGE = 4
ROW_TILE = 512
MOE_BM = 256
ROUTER_LANES = 128
VMEM_LIMIT = 56 * 1024 * 1024


def _dot(a, b):
    return jnp.dot(a, b, preferred_element_type=F32)


def _dot_split(a, b_bf):
    hi = a.astype(BF16)
    lo = (a - hi.astype(F32)).astype(BF16)
    return _dot(hi, b_bf) + _dot(lo, b_bf)


def _dot_nt(a, b):
    return lax.dot_general(a, b, (((1,), (1,)), ((), ())), preferred_element_type=F32)


def _silu(x):
    return x * (1.0 / (1.0 + jnp.exp(-x)))


def _params(sem):
    return pltpu.CompilerParams(dimension_semantics=sem, vmem_limit_bytes=VMEM_LIMIT)


ROW_SUB = D_MODEL // LANES


def _load_rows(ref, n):
    return jnp.concatenate([ref[pl.ds(j, n, stride=ROW_SUB), :] for j in range(ROW_SUB)], axis=1)


def _store_rows(ref, val):
    n = val.shape[0]
    for j in range(ROW_SUB):
        ref[pl.ds(j, n, stride=ROW_SUB), :] = val[:, j * LANES:(j + 1) * LANES]


def _in_proj_body(combine, *refs):
    if combine:
        x_ref, y0_ref, y1_ref, gate_ref, g_ref, w_ref, xo_ref, hg_ref, qkv_ref, u_ref = refs
        gate = gate_ref[...]
        n = x_ref.shape[0]
        x = x_ref[...] + gate[:, 0:1] * _load_rows(y0_ref, n) + gate[:, 1:2] * _load_rows(y1_ref, n)
        xo_ref[...] = x
    else:
        x_ref, g_ref, w_ref, hg_ref, qkv_ref, u_ref = refs
        x = x_ref[...]
    h = x * lax.rsqrt(jnp.mean(x * x, axis=-1, keepdims=True) + NORM_EPS) * g_ref[...]
    hb = h.astype(BF16)
    hg_ref[...] = _dot(hb, w_ref[:, :HG_COLS])
    qkv_ref[...] = _dot(hb, w_ref[:, HG_COLS:HG_COLS + QKV_COLS])
    u_ref[...] = _dot(hb, w_ref[:, HG_COLS + QKV_COLS:])


def _in_proj(x, norm_g, w_bf, moe=None):
    T = x.shape[0]
    tm = min(ROW_TILE, T)
    nt = T // tm
    row = lambda i: (i, 0)
    fixed = lambda i: (0, 0)
    in_specs = [pl.BlockSpec((tm, D_MODEL), row)]
    args = [x]
    out_shape = []
    out_specs = []
    if moe is not None:
        ys, gate = moe
        in_specs += [pl.BlockSpec((tm * ROW_SUB, LANES), row),
                     pl.BlockSpec((tm * ROW_SUB, LANES), lambda i: (i + nt, 0)),
                     pl.BlockSpec((tm, LANES), row)]
        args += [ys, ys, gate]
        out_shape.append(jax.ShapeDtypeStruct((T, D_MODEL), F32))
        out_specs.append(pl.BlockSpec((tm, D_MODEL), row))
    in_specs += [pl.BlockSpec((1, D_MODEL), fixed), pl.BlockSpec((D_MODEL, IN_WIDTH), fixed)]
    args += [norm_g.reshape(1, D_MODEL), w_bf]
    out_shape += [jax.ShapeDtypeStruct((T, HG_COLS), F32), jax.ShapeDtypeStruct((T, QKV_COLS), F32),
                  jax.ShapeDtypeStruct((T, S5_WIDTH), F32)]
    out_specs += [pl.BlockSpec((tm, HG_COLS), row), pl.BlockSpec((tm, QKV_COLS), row),
                  pl.BlockSpec((tm, S5_WIDTH), row)]
    return pl.pallas_call(
        functools.partial(_in_proj_body, moe is not None),
        grid=(nt,), in_specs=in_specs, out_specs=out_specs, out_shape=out_shape,
        compiler_params=_params(("parallel",)), name="in_proj")(*args)


def _log_forget_and_key(z, log_lb, log_1m_lb):
    log_sig = jnp.minimum(z, 0.0) - jnp.log1p(jnp.exp(-jnp.abs(z)))
    a = log_1m_lb + log_sig
    log_f = jnp.maximum(a, log_lb) + jnp.log1p(jnp.exp(-jnp.abs(a - log_lb)))
    return log_f, jnp.exp(a - z)


def _split3(x):
    hi = x.astype(BF16)
    r = x - hi.astype(F32)
    mid = r.astype(BF16)
    lo = (r - mid.astype(F32)).astype(BF16)
    return hi, mid, lo


def _hgrn_body(hg_ref, lb_ref, gn_ref, o_ref, of_s, ob_s, st_s, kv_s):
    S = hg_ref.shape[0]
    n_blk = S // HG_BLOCK
    n_sub = HG_BLOCK // HG_CHUNK
    row = lax.broadcasted_iota(jnp.int32, (HG_BLOCK, HG_BLOCK), 0)
    col = lax.broadcasted_iota(jnp.int32, (HG_BLOCK, HG_BLOCK), 1)
    same = (row // HG_CHUNK) == (col // HG_CHUNK)
    lane_head = lax.broadcasted_iota(jnp.int32, (1, HG_WIDTH), 1) // HEAD_DIM
    col_sub = lax.broadcasted_iota(jnp.int32, (HG_WIDTH, HG_BLOCK), 1) // HG_CHUNK
    same_bf = jnp.where(same, 1.0, 0.0).astype(BF16)

    st_s[...] = jnp.zeros(st_s.shape, F32)

    def block(d, j, out_s):
        r0 = pl.multiple_of(j * HG_BLOCK, HG_BLOCK)
        rows = pl.ds(r0, HG_BLOCK)
        tri = same & ((col <= row) if d == 0 else (col >= row))
        z = hg_ref[rows, (3 + d) * HG_WIDTH:(4 + d) * HG_WIDTH]
        lf, k = _log_forget_and_key(z, lb_ref[2 * d:2 * d + 1, :], lb_ref[2 * d + 1:2 * d + 2, :])
        a_mat = jnp.concatenate([jnp.where(tri, 1.0, 0.0).astype(BF16), same_bf], axis=0)
        hi, mid, lo = _split3(lf)
        acc = _dot(a_mat, hi) + _dot(a_mat, mid) + _dot(a_mat, lo)
        b = acc[:HG_BLOCK]
        tot = acc[HG_BLOCK:]
        q = _silu(hg_ref[rows, 0:HG_WIDTH])
        v = hg_ref[rows, HG_WIDTH:2 * HG_WIDTH]
        q_dec = q * jnp.exp(b)
        k_dec = (k * jnp.exp(-b)).astype(BF16)
        k_end = (k * jnp.exp(tot - b)).astype(BF16)
        dec = jnp.exp(tot)
        q_heads = jnp.concatenate(
            [jnp.where(lane_head == h, q_dec, 0.0).astype(BF16) for h in range(HG_HEADS)], axis=0)
        v_t = v.T
        v_sub = jnp.concatenate(
            [jnp.where(col_sub == c, v_t, 0.0).astype(BF16) for c in range(n_sub)], axis=0)
        kv_s[d] = _dot(v_sub, k_end)
        yield
        scores = _dot_nt(q_heads, k_dec)
        o_blk = jnp.zeros((HG_BLOCK, HG_WIDTH), F32)
        for h in range(HG_HEADS):
            p_h = jnp.where(tri, scores[h * HG_BLOCK:(h + 1) * HG_BLOCK], 0.0).astype(BF16)
            o_blk += _dot(p_h, jnp.where(lane_head == h, v, 0.0).astype(BF16))
        yield
        o_inter = [None] * n_sub
        order = range(n_sub) if d == 0 else range(n_sub - 1, -1, -1)
        for c in order:
            q_c = jnp.concatenate(
                [q_heads[h * HG_BLOCK + c * HG_CHUNK:h * HG_BLOCK + (c + 1) * HG_CHUNK]
                 for h in range(HG_HEADS)], axis=0)
            st = st_s[d]
            o_all = _dot_nt(q_c, st.astype(BF16))
            o_c = jnp.zeros((HG_CHUNK, HG_WIDTH), F32)
            for h in range(HG_HEADS):
                o_c += jnp.where(lane_head == h, o_all[h * HG_CHUNK:(h + 1) * HG_CHUNK], 0.0)
            o_inter[c] = o_c
            st_s[d] = dec[c * HG_CHUNK:c * HG_CHUNK + 1, :] * st + kv_s[d, c * HG_WIDTH:(c + 1) * HG_WIDTH, :]
            yield
        out_s[rows, :] = o_blk + jnp.concatenate(o_inter, axis=0)

    def step(j, carry):
        stages = [block(0, j, of_s), block(1, n_blk - 1 - j, ob_s)]
        while stages:
            stages = [g for g in stages if next(g, True) is None]
        return carry

    lax.fori_loop(0, n_blk, step, 0)

    r2 = lax.broadcasted_iota(jnp.int32, (HG_WIDTH, HG_WIDTH), 0) // HEAD_DIM
    c2 = lax.broadcasted_iota(jnp.int32, (HG_WIDTH, HG_WIDTH), 1) // HEAD_DIM
    head_mean = jnp.where(r2 == c2, 1.0 / HEAD_DIM, 0.0).astype(BF16)

    def finish(j, carry):
        rows = pl.ds(pl.multiple_of(j * HG_BLOCK, HG_BLOCK), HG_BLOCK)
        o = of_s[rows, :] + ob_s[rows, :]
        ms = _dot_split(o * o, head_mean)
        y = o * lax.rsqrt(ms + NORM_EPS) * gn_ref[...]
        o_ref[rows, :] = y * _silu(hg_ref[rows, 2 * HG_WIDTH:3 * HG_WIDTH])
        return carry

    lax.fori_loop(0, n_blk, finish, 0)


def _hgrn(hg, lb_rows, gn, B, S):
    return pl.pallas_call(
        _hgrn_body,
        grid=(B,),
        in_specs=[pl.BlockSpec((S, HG_COLS), lambda b: (b, 0)),
                  pl.BlockSpec((4, HG_WIDTH), lambda b: (0, 0)),
                  pl.BlockSpec((1, HG_WIDTH), lambda b: (0, 0))],
        out_specs=pl.BlockSpec((S, HG_WIDTH), lambda b: (b, 0)),
        out_shape=jax.ShapeDtypeStruct((B * S, HG_WIDTH), F32),
        scratch_shapes=[pltpu.VMEM((S, HG_WIDTH), F32), pltpu.VMEM((S, HG_WIDTH), F32),
                        pltpu.VMEM((2, HG_WIDTH, HG_WIDTH), F32),
                        pltpu.VMEM((2, HG_BLOCK // HG_CHUNK * HG_WIDTH, HG_WIDTH), F32)],
        compiler_params=_params(("parallel",)), name="hgrn2")(hg, lb_rows, gn)


def _attn_body(qkv_ref, cs_ref, gain_ref, o_ref, q_s, k_s, vlo_s, vhi_s):
    S = qkv_ref.shape[0]
    n_qb = S // ATTN_QB
    lane = lax.broadcasted_iota(jnp.int32, (1, LANES), 1)
    lo = lane < HEAD_DIM
    even = (lane % 2) == 0
    r2 = lax.broadcasted_iota(jnp.int32, (LANES, LANES), 0) // HEAD_DIM
    c2 = lax.broadcasted_iota(jnp.int32, (LANES, LANES), 1) // HEAD_DIM
    head_mean = jnp.where(r2 == c2, 1.0 / HEAD_DIM, 0.0).astype(BF16)
    n_pairs = ATTN_WIDTH // LANES

    def norm_rope(x, gain, cos, sin):
        ms = _dot_split(x * x, head_mean)
        y = x * lax.rsqrt(ms + NORM_EPS) * gain
        swapped = jnp.where(even, pltpu.roll(y, LANES - 1, 1), pltpu.roll(y, 1, 1))
        return y * cos + swapped * sin

    def prep(j, carry):
        rows = pl.ds(pl.multiple_of(j * ATTN_QB, ATTN_QB), ATTN_QB)
        cos = cs_ref[0, rows, :]
        sin = cs_ref[1, rows, :]
        for m in range(n_pairs):
            qm = norm_rope(qkv_ref[rows, m * LANES:(m + 1) * LANES], gain_ref[0:1, :], cos, sin)
            qm = qm * (HEAD_DIM ** -0.5)
            q_s[m, 0, rows, :] = jnp.where(lo, qm, 0.0).astype(BF16)
            q_s[m, 1, rows, :] = jnp.where(lo, 0.0, qm).astype(BF16)
        kk = norm_rope(qkv_ref[rows, ATTN_WIDTH:ATTN_WIDTH + LANES], gain_ref[1:2, :], cos, sin)
        kr = pltpu.roll(kk, HEAD_DIM, 1)
        k_s[0, rows, :] = jnp.where(lo, kk, kr).astype(BF16)
        k_s[1, rows, :] = jnp.where(lo, kr, kk).astype(BF16)
        vv = qkv_ref[rows, ATTN_WIDTH + LANES:ATTN_WIDTH + 2 * LANES]
        vr = pltpu.roll(vv, HEAD_DIM, 1)
        ones_hi = jnp.where(lane == HEAD_DIM, 1.0, 0.0)
        ones_lo = jnp.where(lane == 0, 1.0, 0.0)
        vlo_s[0, rows, :] = jnp.where(lo, vv, ones_hi).astype(BF16)
        vhi_s[0, rows, :] = jnp.where(lo, ones_lo, vr).astype(BF16)
        vlo_s[1, rows, :] = jnp.where(lo, vr, ones_hi).astype(BF16)
        vhi_s[1, rows, :] = jnp.where(lo, ones_lo, vv).astype(BF16)
        return carry

    lax.fori_loop(0, n_qb, prep, 0)

    def attend(j, carry):
        rows = pl.ds(pl.multiple_of(j * ATTN_QB, ATTN_QB), ATTN_QB)
        for m in range(n_pairs):
            kv = m // (n_pairs // 2)
            q2 = jnp.concatenate([q_s[m, 0, rows, :], q_s[m, 1, rows, :]], axis=0)
            s = _dot_nt(q2, k_s[kv])
            p = jnp.exp((s - jnp.max(s, axis=-1, keepdims=True)).astype(BF16))
            o_lo = _dot(p[:ATTN_QB], vlo_s[kv])
            o_hi = _dot(p[ATTN_QB:], vhi_s[kv])
            o_ref[rows, m * LANES:(m + 1) * LANES] = jnp.where(
                lo, o_lo / o_lo[:, HEAD_DIM:HEAD_DIM + 1], o_hi / o_hi[:, 0:1])
        return carry

    lax.fori_loop(0, n_qb, attend, 0)


def _attention(qkv, cs, gains, B, S):
    n_pairs = ATTN_WIDTH // LANES
    return pl.pallas_call(
        _attn_body,
        grid=(B,),
        in_specs=[pl.BlockSpec((S, QKV_COLS), lambda b: (b, 0)),
                  pl.BlockSpec((2, S, LANES), lambda b: (0, 0, 0)),
                  pl.BlockSpec((2, LANES), lambda b: (0, 0))],
        out_specs=pl.BlockSpec((S, ATTN_WIDTH), lambda b: (b, 0)),
        out_shape=jax.ShapeDtypeStruct((B * S, ATTN_WIDTH), F32),
        scratch_shapes=[pltpu.VMEM((n_pairs, 2, S, LANES), BF16), pltpu.VMEM((2, S, LANES), BF16),
                        pltpu.VMEM((2, S, LANES), BF16), pltpu.VMEM((2, S, LANES), BF16)],
        compiler_params=_params(("parallel",)), name="gqa_attention")(qkv, cs, gains)


def _rope_tables(S):
    rows_count = S // GRID_W
    rows = jnp.repeat(jnp.arange(rows_count, dtype=F32), GRID_W)
    cols = jnp.tile(jnp.arange(GRID_W, dtype=F32), rows_count)
    pairs = HEAD_DIM // 4
    freqs = jnp.power(jnp.float32(ROPE_THETA), -jnp.arange(pairs, dtype=F32) / pairs)
    ang = jnp.concatenate([rows[:, None] * freqs, cols[:, None] * freqs], axis=-1)
    cos = jnp.repeat(jnp.cos(ang), 2, axis=-1)
    sin = jnp.repeat(jnp.sin(ang), 2, axis=-1) * jnp.tile(jnp.array([-1.0, 1.0], F32), HEAD_DIM // 2)
    reps = LANES // HEAD_DIM
    return jnp.stack([jnp.tile(cos, (1, reps)), jnp.tile(sin, (1, reps))])


def _s5_body(uf_ref, ub_ref, perm_ref, bmat_ref, lam_ref, cmat_ref, yf_ref, yb_ref, st_s, x_s):
    B, lt, _ = uf_ref.shape
    half = S5_REAL // 2

    @pl.when(pl.program_id(0) == 0)
    def _():
        st_s[...] = jnp.zeros(st_s.shape, F32)

    n_tiles = half // LANES

    def direction(d, u_ref, y_ref):
        u = u_ref[...].reshape(B * lt, S5_WIDTH).astype(BF16)
        u_tm = _dot(perm_ref[0], u).astype(BF16)
        bu = _dot(u_tm, bmat_ref[d])
        for c in range(2 * n_tiles):
            x_s[d, c] = bu[:, c * LANES:(c + 1) * LANES]
        yield
        order = range(lt) if d == 0 else range(lt - 1, -1, -1)
        re_t = lambda c: slice(c * LANES, (c + 1) * LANES)
        im_t = lambda c: slice(half + c * LANES, half + (c + 1) * LANES)
        xr = [st_s[d, :, re_t(c)] for c in range(n_tiles)]
        xi = [st_s[d, :, im_t(c)] for c in range(n_tiles)]
        for n, t in enumerate(order):
            slab = slice(t * B, (t + 1) * B)
            for c in range(n_tiles):
                lr = lam_ref[d, :, re_t(c)]
                li = lam_ref[d, :, im_t(c)]
                nr = lr * xr[c] - li * xi[c] + x_s[d, c, slab, :]
                ni = lr * xi[c] + li * xr[c] + x_s[d, n_tiles + c, slab, :]
                xr[c], xi[c] = nr, ni
                x_s[d, c, slab, :] = nr
                x_s[d, n_tiles + c, slab, :] = ni
            if n % S5_STAGE == S5_STAGE - 1:
                yield
        for c in range(n_tiles):
            st_s[d, :, re_t(c)] = xr[c]
            st_s[d, :, im_t(c)] = xi[c]
        xs = jnp.concatenate([x_s[d, c].astype(BF16) for c in range(2 * n_tiles)], axis=1)
        y_tm = _dot(xs, cmat_ref[d])
        hi = y_tm.astype(BF16)
        lo = (y_tm - hi.astype(F32)).astype(BF16)
        y_ref[...] = (_dot(perm_ref[1], hi) + _dot(perm_ref[1], lo)).reshape(B, lt, S5_WIDTH)

    stages = [direction(0, uf_ref, yf_ref), direction(1, ub_ref, yb_ref)]
    while stages:
        stages = [g for g in stages if next(g, True) is None]


def _s5_scan(u, bmat, lam_b, cmat, B, S):
    lt = min(S5_LT, S)
    n = S // lt
    fwd = lambda j: (0, j, 0)
    bwd = lambda j: (0, n - 1 - j, 0)
    fixed = lambda j: (0, 0, 0)
    r = jnp.arange(B * lt, dtype=jnp.int32)
    to_time_major = jax.nn.one_hot((r % B) * lt + r // B, B * lt, dtype=BF16)
    perm = jnp.stack([to_time_major, to_time_major.T])
    return pl.pallas_call(
        _s5_body,
        grid=(n,),
        in_specs=[pl.BlockSpec((B, lt, S5_WIDTH), fwd), pl.BlockSpec((B, lt, S5_WIDTH), bwd),
                  pl.BlockSpec((2, B * lt, B * lt), fixed),
                  pl.BlockSpec((2, S5_WIDTH, S5_REAL), fixed), pl.BlockSpec((2, B, S5_REAL), fixed),
                  pl.BlockSpec((2, S5_REAL, S5_WIDTH), fixed)],
        out_specs=[pl.BlockSpec((B, lt, S5_WIDTH), fwd), pl.BlockSpec((B, lt, S5_WIDTH), bwd)],
        out_shape=[jax.ShapeDtypeStruct((B, S, S5_WIDTH), F32)] * 2,
        scratch_shapes=[pltpu.VMEM((2, B, S5_REAL), F32), pltpu.VMEM((2, S5_REAL // LANES, B * lt, LANES), F32)],
        compiler_params=_params(("arbitrary",)), name="s5_scan")(u, u, perm, bmat, lam_b, cmat)


def _s5_operators(a_re, a_im, log_dt, b_re, b_im, c_re, c_im, B):
    dt = jnp.exp(log_dt.astype(F32))[..., None]
    ar, ai = a_re.astype(F32), a_im.astype(F32)
    mag = jnp.exp(ar * dt)
    lr, li = mag * jnp.cos(ai * dt), mag * jnp.sin(ai * dt)
    den = ar * ar + ai * ai
    fr = ((lr - 1.0) * ar + li * ai) / den
    fi = (li * ar - (lr - 1.0) * ai) / den
    br, bi = b_re.astype(F32), b_im.astype(F32)
    bbr = fr[..., None] * br - fi[..., None] * bi
    bbi = fr[..., None] * bi + fi[..., None] * br
    eye = jnp.eye(S5_GROUPS, dtype=F32)
    in_blk = lambda m: jnp.einsum('zgpc,gh->zgchp', m, eye).reshape(2, S5_WIDTH, S5_REAL // 2)
    out_blk = lambda m: jnp.einsum('zgcp,gh->zgphc', m, eye).reshape(2, S5_REAL // 2, S5_WIDTH)
    bmat = jnp.concatenate([in_blk(bbr), in_blk(bbi)], axis=-1).astype(BF16)
    cmat = jnp.concatenate([out_blk(c_re.astype(F32)), -out_blk(c_im.astype(F32))], axis=1).astype(BF16)
    lam = jnp.concatenate([lr.reshape(2, 1, -1), li.reshape(2, 1, -1)], axis=-1)
    return bmat, jnp.broadcast_to(lam, (2, B, S5_REAL)), cmat


def _gelu_tanh(x):
    return 0.5 * x * (1.0 + jnp.tanh(math.sqrt(2.0 / math.pi) * (x + 0.044715 * (x * x * x))))


def _out_proj_body(x_ref, oa_ref, ob_ref, u_ref, yf_ref, yb_ref, dskip_ref, wglu_ref, bglu_ref, wout_ref,
                   g2_ref, wr_ref, br_ref, x1_ref, h2_ref, eid_ref, gate_ref):
    y = dskip_ref[...] * u_ref[...] + yf_ref[...] + yb_ref[...]
    z = _gelu_tanh(y)
    t = _dot(z.astype(BF16), wglu_ref[...]) + bglu_ref[...]
    oc = z * (1.0 / (1.0 + jnp.exp(-t)))
    mix = _dot(oa_ref[...].astype(BF16), wout_ref[0:HG_WIDTH, :])
    mix += _dot(ob_ref[...].astype(BF16), wout_ref[HG_WIDTH:HG_WIDTH + ATTN_WIDTH, :])
    mix += _dot(oc.astype(BF16), wout_ref[HG_WIDTH + ATTN_WIDTH:, :])
    x1 = x_ref[...] + mix
    x1_ref[...] = x1
    h2 = x1 * lax.rsqrt(jnp.mean(x1 * x1, axis=-1, keepdims=True) + NORM_EPS) * g2_ref[...]
    _store_rows(h2_ref, h2)
    h_hi = h2.astype(BF16)
    h_lo = (h2 - h_hi.astype(F32)).astype(BF16)
    logits = _dot(h_hi, wr_ref[0]) + (_dot(h_lo, wr_ref[0]) + _dot(h_hi, wr_ref[1])) + br_ref[...]
    lane = lax.broadcasted_iota(jnp.int32, logits.shape, 1)
    lane_f = lane.astype(F32)
    neg = jnp.float32(-jnp.inf)
    is_g = lane < N_GROUPS
    gl = jnp.where(is_g, logits, neg)
    gmax = jnp.max(gl, axis=-1, keepdims=True)
    gsel = jnp.min(jnp.where(gl == gmax, lane_f, float(ROUTER_LANES)), axis=-1, keepdims=True)
    p_group = 1.0 / jnp.sum(jnp.where(is_g, jnp.exp(logits - gmax), 0.0), axis=-1, keepdims=True)
    first = N_GROUPS + PER_GROUP * gsel
    el = jnp.where((lane_f >= first) & (lane_f < first + PER_GROUP), logits, neg)
    v1 = jnp.max(el, axis=-1, keepdims=True)
    i1 = jnp.min(jnp.where(el == v1, lane_f, float(ROUTER_LANES)), axis=-1, keepdims=True)
    el2 = jnp.where(lane_f == i1, neg, el)
    v2 = jnp.max(el2, axis=-1, keepdims=True)
    i2 = jnp.min(jnp.where(el2 == v2, lane_f, float(ROUTER_LANES)), axis=-1, keepdims=True)
    e2 = jnp.exp(v2 - v1)
    w1 = p_group / (1.0 + e2)
    w2 = p_group * e2 / (1.0 + e2)
    eid_ref[...] = jnp.where(lane == 0, i1 - N_GROUPS, jnp.where(lane == 1, i2 - N_GROUPS, 0.0)).astype(jnp.int32)
    gate_ref[...] = jnp.where(lane == 0, w1, jnp.where(lane == 1, w2, 0.0))


def _out_proj(x, oa, ob, u, yf, yb, dskip, wglu_bf, bglu, wout_bf, g2, wr, br):
    T = x.shape[0]
    tm = min(ROW_TILE, T)
    row = lambda i: (i, 0)
    blk = lambda w: pl.BlockSpec((tm, w), row)
    full = lambda a: pl.BlockSpec(a.shape, lambda i: (0,) * a.ndim)
    consts = [dskip, wglu_bf, bglu, wout_bf, g2, wr, br]
    return pl.pallas_call(
        _out_proj_body,
        grid=(T // tm,),
        in_specs=[blk(D_MODEL), blk(HG_WIDTH), blk(ATTN_WIDTH), blk(S5_WIDTH), blk(S5_WIDTH), blk(S5_WIDTH)]
        + [full(a) for a in consts],
        out_specs=[blk(D_MODEL), pl.BlockSpec((tm * ROW_SUB, LANES), row), blk(ROUTER_LANES), blk(ROUTER_LANES)],
        out_shape=[jax.ShapeDtypeStruct((T, D_MODEL), F32), jax.ShapeDtypeStruct((T * ROW_SUB, LANES), F32),
                   jax.ShapeDtypeStruct((T, ROUTER_LANES), jnp.int32),
                   jax.ShapeDtypeStruct((T, ROUTER_LANES), F32)],
        compiler_params=_params(("parallel",)), name="out_proj_router")(x, oa, ob, u, yf, yb, *consts)


def _moe_body(bm, blk_e_ref, n_used_ref, tok_ref, tok_next_ref, dst_ref, h2_hbm, wg_ref, wu_ref, wd_ref,
              out_hbm, xbuf, obuf, wg_s, wu_s, wd_s, gsem, ssem):
    i = pl.program_id(0)
    n_used = n_used_ref[0]
    slot = i % 2

    @pl.when((i == 0) | (blk_e_ref[i] != blk_e_ref[jnp.maximum(i - 1, 0)]))
    def _():
        wg_s[...] = wg_ref[...].astype(BF16)
        wu_s[...] = wu_ref[...].astype(BF16)
        wd_s[...] = wd_ref[...].astype(BF16)

    def gather(idx_ref, s):
        def issue(r, carry):
            pltpu.make_async_copy(h2_hbm.at[idx_ref[0, 0, r]],
                                  xbuf.at[s, pl.ds(pl.multiple_of(r * ROW_SUB, ROW_SUB), ROW_SUB)],
                                  gsem.at[s]).start()
            return carry
        lax.fori_loop(0, bm, issue, 0, unroll=8)

    def scatter_wait(s):
        pltpu.make_async_copy(obuf.at[s], obuf.at[s], ssem.at[s]).wait()

    @pl.when(i == 0)
    def _():
        gather(tok_ref, 0)
        obuf[0] = jnp.zeros(obuf.shape[1:], F32)
        n_assign_rows = out_hbm.shape[0] - 2 * bm

        def clear(r, carry):
            src = obuf.at[0, pl.ds(pl.multiple_of((r % bm) * ROW_SUB, ROW_SUB), ROW_SUB)]
            pltpu.make_async_copy(src, out_hbm.at[n_assign_rows + r], ssem.at[0]).start()
            return carry
        lax.fori_loop(0, 2 * bm, clear, 0, unroll=8)
        scatter_wait(0)
        scatter_wait(0)

    @pl.when(i + 1 < n_used)
    def _():
        gather(tok_next_ref, 1 - slot)

    @pl.when(i < n_used)
    def _():
        pltpu.make_async_copy(xbuf.at[slot], xbuf.at[slot], gsem.at[slot]).wait()
        xb = _load_rows(xbuf.at[slot], bm).astype(BF16)
        g = _dot(xb, wg_s[...])
        u = _dot(xb, wu_s[...])
        _store_rows(obuf.at[slot], _dot((_silu(g) * u).astype(BF16), wd_s[...]))

        def issue(r, carry):
            pltpu.make_async_copy(obuf.at[slot, pl.ds(pl.multiple_of(r * ROW_SUB, ROW_SUB), ROW_SUB)],
                                  out_hbm.at[dst_ref[0, 0, r]], ssem.at[slot]).start()
            return carry
        lax.fori_loop(0, bm, issue, 0, unroll=8)

        @pl.when(i > 0)
        def _():
            scatter_wait(1 - slot)

        @pl.when(i == n_used - 1)
        def _():
            scatter_wait(slot)


def _moe_experts(h2, tables, wg, wu, wd, layer, bm, T):
    row_tok, row_dst, blk_e, n_used = tables
    n_blocks = blk_e.shape[0]
    tok3 = row_tok.reshape(n_blocks, 1, bm)
    dst3 = row_dst.reshape(n_blocks, 1, bm)
    smem_blk = lambda f: pl.BlockSpec((1, 1, bm), f, memory_space=pltpu.SMEM)
    wspec = lambda shape: pl.BlockSpec((None, None) + shape, lambda i, be, nu: (layer, be[i], 0, 0))
    out_rows = TOP_K * T + 2 * bm
    out = pl.pallas_call(
        functools.partial(_moe_body, bm),
        grid_spec=pltpu.PrefetchScalarGridSpec(
            num_scalar_prefetch=2, grid=(n_blocks,),
            in_specs=[smem_blk(lambda i, be, nu: (i, 0, 0)),
                      smem_blk(lambda i, be, nu: (jnp.minimum(i + 1, n_blocks - 1), 0, 0)),
                      smem_blk(lambda i, be, nu: (i, 0, 0)),
                      pl.BlockSpec(memory_space=pl.ANY),
                      wspec((D_MODEL, EXPERT_FF)), wspec((D_MODEL, EXPERT_FF)), wspec((EXPERT_FF, D_MODEL))],
            out_specs=pl.BlockSpec(memory_space=pl.ANY),
            scratch_shapes=[pltpu.VMEM((2, bm * ROW_SUB, LANES), F32), pltpu.VMEM((2, bm * ROW_SUB, LANES), F32),
                            pltpu.VMEM((D_MODEL, EXPERT_FF), BF16), pltpu.VMEM((D_MODEL, EXPERT_FF), BF16),
                            pltpu.VMEM((EXPERT_FF, D_MODEL), BF16),
                            pltpu.SemaphoreType.DMA((2,)), pltpu.SemaphoreType.DMA((2,))]),
        out_shape=jax.ShapeDtypeStruct((out_rows, ROW_SUB, LANES), F32),
        compiler_params=_params(("arbitrary",)), name="moe_experts")(
            blk_e, n_used, tok3, tok3, dst3, h2.reshape(T, ROW_SUB, LANES), wg, wu, wd)
    return out.reshape(out_rows * ROW_SUB, LANES)


def _dispatch_tables(eid, T, bm):
    n_assign = T * TOP_K
    flat_e = eid.reshape(n_assign)
    onehot = (flat_e[:, None] == jnp.arange(N_EXPERTS, dtype=jnp.int32)[None, :]).astype(jnp.int32)
    csum = jnp.cumsum(onehot, axis=0)
    rank = jnp.sum(csum * onehot, axis=1) - 1
    counts = csum[-1]
    padded = ((counts + bm - 1) // bm) * bm
    pends = jnp.cumsum(padded)
    pstarts = pends - padded
    dest = (jnp.sum(onehot * pstarts[None, :], axis=1) + rank).astype(jnp.int32)
    n_blocks = n_assign // bm + N_EXPERTS
    assign = jnp.arange(n_assign, dtype=jnp.int32)
    row_assign = jnp.zeros((n_blocks * bm,), jnp.int32).at[dest].set(
        assign + 1, unique_indices=True, mode='promise_in_bounds') - 1
    valid = row_assign >= 0
    row = jnp.arange(n_blocks * bm, dtype=jnp.int32)
    spare = TOP_K * T + ((row // bm) % 2) * bm + row % bm
    row_tok = jnp.where(valid, row_assign // TOP_K, 0)
    row_dst = jnp.where(valid, (row_assign % TOP_K) * T + row_assign // TOP_K, spare)
    blk_start = jnp.arange(n_blocks, dtype=jnp.int32) * bm
    blk_e = jnp.sum((blk_start[:, None] >= pends[None, :]).astype(jnp.int32), axis=1)
    blk_e = jnp.minimum(blk_e, N_EXPERTS - 1)
    n_used = (pends[-1] // bm).astype(jnp.int32).reshape(1)
    return row_tok, row_dst, blk_e, n_used


def _moe(h2, eid, wg_bf, wu_bf, wd_bf, layer):
    T = h2.shape[0] // ROW_SUB
    bm = min(MOE_BM, T)
    return _moe_experts(h2, _dispatch_tables(eid, T, bm), wg_bf, wu_bf, wd_bf, layer, bm, T)


def _final_body(x_ref, y0_ref, y1_ref, gate_ref, g_ref, o_ref):
    gate = gate_ref[...]
    n = x_ref.shape[0]
    x = x_ref[...] + gate[:, 0:1] * _load_rows(y0_ref, n) + gate[:, 1:2] * _load_rows(y1_ref, n)
    o_ref[...] = x * lax.rsqrt(jnp.mean(x * x, axis=-1, keepdims=True) + NORM_EPS) * g_ref[...]


def _final_norm(x, ys, gate, g):
    T = x.shape[0]
    tm = min(ROW_TILE, T)
    nt = T // tm
    row = lambda i: (i, 0)
    return pl.pallas_call(
        _final_body,
        grid=(nt,),
        in_specs=[pl.BlockSpec((tm, D_MODEL), row), pl.BlockSpec((tm * ROW_SUB, LANES), row),
                  pl.BlockSpec((tm * ROW_SUB, LANES), lambda i: (i + nt, 0)), pl.BlockSpec((tm, LANES), row),
                  pl.BlockSpec((1, D_MODEL), lambda i: (0, 0))],
        out_specs=pl.BlockSpec((tm, D_MODEL), row),
        out_shape=jax.ShapeDtypeStruct((T, D_MODEL), F32),
        compiler_params=_params(("parallel",)), name="combine_final_norm")(x, ys, ys, gate, g.reshape(1, D_MODEL))


def kernel(x, norm1_g, w_in, hgrn_lower_bounds, hgrn_norm_g, attn_q_norm_g, attn_k_norm_g, s5_a_re, s5_a_im, s5_log_dt, s5_b_re, s5_b_im, s5_c_re, s5_c_im, s5_d, s5_w_glu, s5_b_glu, w_out, norm2_g, router_group_w, router_group_b, router_expert_w, router_expert_b, expert_w_gate, expert_w_up, expert_w_down, final_norm_g):
    B, S, D = x.shape
    T = B * S
    depth = w_in.shape[0]
    cs = _rope_tables(S)
    lb_all = jnp.cumsum(jax.nn.softmax(hgrn_lower_bounds.astype(F32), axis=0), axis=0)
    lb_all = lb_all - lb_all[0:1]
    wg_bf, wu_bf, wd_bf = expert_w_gate, expert_w_up, expert_w_down
    reps = LANES // HEAD_DIM
    pad = ROUTER_LANES - N_GROUPS - N_EXPERTS

    xt = x.reshape(T, D)
    moe = None
    for l in range(depth):
        if moe is None:
            hg, qkv, u = _in_proj(xt, norm1_g[l], w_in[l].astype(BF16))
        else:
            xt, hg, qkv, u = _in_proj(xt, norm1_g[l], w_in[l].astype(BF16), moe)
        lb = lb_all[l]
        lb_rows = jnp.stack([jnp.log(lb[0]), jnp.log1p(-lb[0]), jnp.log(lb[1]), jnp.log1p(-lb[1])])
        oa = _hgrn(hg, lb_rows, jnp.tile(hgrn_norm_g[l].astype(F32), HG_HEADS).reshape(1, HG_WIDTH), B, S)
        gains = jnp.stack([jnp.tile(attn_q_norm_g[l].astype(F32), reps), jnp.tile(attn_k_norm_g[l].astype(F32), reps)])
        ob = _attention(qkv, cs, gains, B, S)
        bmat, lam_b, cmat = _s5_operators(s5_a_re[l], s5_a_im[l], s5_log_dt[l], s5_b_re[l], s5_b_im[l],
                                          s5_c_re[l], s5_c_im[l], B)
        yf, yb = _s5_scan(u.reshape(B, S, S5_WIDTH), bmat, lam_b, cmat, B, S)
        wr = jnp.concatenate([router_group_w[l].astype(F32), router_expert_w[l].astype(F32),
                              jnp.zeros((D, pad), F32)], axis=1)
        wr_hi = wr.astype(BF16)
        wr = jnp.stack([wr_hi, (wr - wr_hi.astype(F32)).astype(BF16)])
        br =jnp.concatenate([router_group_b[l].astype(F32), router_expert_b[l].astype(F32),
                              jnp.zeros((pad,), F32)]).reshape(1, ROUTER_LANES)
        xt, h2, eid, gate = _out_proj(
            xt, oa, ob, u, yf.reshape(T, S5_WIDTH), yb.reshape(T, S5_WIDTH),
            s5_d[l].astype(F32).reshape(1, S5_WIDTH), s5_w_glu[l].astype(BF16),
            s5_b_glu[l].astype(F32).reshape(1, S5_WIDTH), w_out[l].astype(BF16),
            norm2_g[l].astype(F32).reshape(1, D), wr, br)
        moe = (_moe(h2, eid[:, :TOP_K], wg_bf, wu_bf, wd_bf, l), gate)
    out = _final_norm(xt, moe[0], moe[1], final_norm_g.astype(F32))
    return out.reshape(B, S, D)
```
